```python
import math
import jax, jax.numpy as jnp
from jax import lax
import numpy as np

D_MODEL = 1024
BATCH = 8
SEQ = 4096
DEPTH = 1
DEC_BATCH = 32
DEC_SEQ = 2048
PAST_LEN = 128

HEAD_DIM = 64
N_META = 16
GRID_W = 64
Q_BLOCK = 128
ROPE_THETA = 10000.0
NORM_EPS = 1e-6
MIX_WIDTH = D_MODEL
A_WIDTH = MIX_WIDTH // 2
A_Q_HEADS = A_WIDTH // HEAD_DIM
A_KV_HEADS = A_Q_HEADS // 4
A_GROUP = A_Q_HEADS // A_KV_HEADS
B_WIDTH = MIX_WIDTH - A_WIDTH
B_V_DIM = 2 * HEAD_DIM
B_HEADS = B_WIDTH // B_V_DIM
A_Q_COLS = A_Q_HEADS * HEAD_DIM
A_KV_COLS = A_KV_HEADS * HEAD_DIM
B_QK_COLS = B_HEADS * 2 * HEAD_DIM
B_V_COLS = B_HEADS * B_V_DIM
IN_COLS = A_Q_COLS + 2 * A_KV_COLS + 2 * B_QK_COLS + B_V_COLS
D_FF = ((8 * D_MODEL // 3 + 255) // 256) * 256
CONV_WIDTH = 3

kernel_name = "hymba_gqa_axial_diffattn_convglu_encoder"


def lambda_init_fn(layer):
    return 0.8 - 0.6 * math.exp(-0.3 * layer)


def rms_norm(x, g):
    x32 = x.astype(jnp.float32)
    y = x32 * lax.rsqrt(jnp.mean(x32 * x32, axis=-1, keepdims=True) + NORM_EPS)
    return (y * g.astype(jnp.float32)).astype(x.dtype)


def rope_rotate(x, angles):
    half = x.shape[-1] // 2
    x32 = x.astype(jnp.float32)
    x1, x2 = x32[..., :half], x32[..., half:]
    shape = (1, angles.shape[0]) + (1,) * (x.ndim - 3) + (half,)
    cos = jnp.cos(angles).reshape(shape)
    sin = jnp.sin(angles).reshape(shape)
    return jnp.concatenate([x1 * cos - x2 * sin, x2 * cos + x1 * sin], axis=-1).astype(x.dtype)


def axial_angles(n_tokens):
    n_rows = n_tokens // GRID_W
    t = jnp.arange(n_rows * GRID_W)
    row = (t // GRID_W).astype(jnp.float32)
    col = (t % GRID_W).astype(jnp.float32)
    axis_dim = HEAD_DIM // 2
    inv = ROPE_THETA ** (-jnp.arange(0, axis_dim, 2, dtype=jnp.float32) / axis_dim)
    ang = jnp.concatenate([row[:, None] * inv[None], col[:, None] * inv[None]], axis=-1)
    meta = jnp.zeros((N_META, HEAD_DIM // 2), jnp.float32)
    return jnp.concatenate([meta, ang], axis=0)


def linear_angles(n_total):
    pos = jnp.arange(n_total, dtype=jnp.float32)
    inv = ROPE_THETA ** (-jnp.arange(0, HEAD_DIM, 2, dtype=jnp.float32) / HEAD_DIM)
    return pos[:, None] * inv[None]


def sweep_query_blocks(q, block_fn):
    b, l = q.shape[0], q.shape[1]
    n_real = l - N_META
    nb = n_real // Q_BLOCK
    out_meta = block_fn(q[:, :N_META])
    qb = q[:, N_META:].reshape((b, nb, Q_BLOCK) + q.shape[2:])
    qb = jnp.moveaxis(qb, 1, 0)
    ob = lax.map(block_fn, qb)
    ob = jnp.moveaxis(ob, 0, 1).reshape((b, n_real) + ob.shape[3:])
    return jnp.concatenate([out_meta, ob], axis=1)


def mixer_sublayer(x, ang_a, ang_b, layer, g_mix, w_in, g_qnorm_a, g_knorm_a,
                   lambda_q1, lambda_k1, lambda_q2, lambda_k2, g_subln, w_out):
    b, l, _ = x.shape
    n = rms_norm(x, g_mix)
    proj = n @ w_in
    cuts = np.cumsum([A_Q_COLS, A_KV_COLS, A_KV_COLS, B_QK_COLS, B_QK_COLS]).tolist()
    qa, ka, va, qb, kb, vb = jnp.split(proj, cuts, axis=-1)
    scale = HEAD_DIM ** -0.5

    qa = qa.reshape(b, l, A_KV_HEADS, A_GROUP, HEAD_DIM)
    ka = ka.reshape(b, l, A_KV_HEADS, HEAD_DIM)
    va = va.reshape(b, l, A_KV_HEADS, HEAD_DIM)
    qa = rope_rotate(rms_norm(qa, g_qnorm_a), ang_a)
    ka = rope_rotate(rms_norm(ka, g_knorm_a), ang_a)

    def gqa_block(qblk):
        s = jnp.einsum('bqhgd,bkhd->bhgqk', qblk, ka, preferred_element_type=jnp.float32) * scale
        p = jax.nn.softmax(s, axis=-1).astype(va.dtype)
        return jnp.einsum('bhgqk,bkhd->bqhgd', p, va)

    oa = sweep_query_blocks(qa, gqa_block).reshape(b, l, A_WIDTH)

    qb = rope_rotate(qb.reshape(b, l, B_HEADS, 2, HEAD_DIM), ang_b)
    kb = rope_rotate(kb.reshape(b, l, B_HEADS, 2, HEAD_DIM), ang_b)
    vb = vb.reshape(b, l, B_HEADS, B_V_DIM)
    lam_init = lambda_init_fn(layer)
    lam = (jnp.exp(jnp.sum(lambda_q1.astype(jnp.float32) * lambda_k1.astype(jnp.float32)))
           - jnp.exp(jnp.sum(lambda_q2.astype(jnp.float32) * lambda_k2.astype(jnp.float32)))
           + lam_init)

    def diff_block(qblk):
        s = jnp.einsum('bqhcd,bkhcd->bhcqk', qblk, kb, preferred_element_type=jnp.float32) * scale
        p = jax.nn.softmax(s, axis=-1)
        a = (p[:, :, 0] - lam * p[:, :, 1]).astype(vb.dtype)
        return jnp.einsum('bhqk,bkhe->bqhe', a, vb)

    ob = sweep_query_blocks(qb, diff_block)
    ob = (rms_norm(ob, g_subln) * (1.0 - lam_init)).astype(x.dtype).reshape(b, l, B_WIDTH)

    mix = jnp.concatenate([oa, ob], axis=-1)
    return x + mix @ w_out


def conv_glu_sublayer(x, g_ffn, w_ff_gate, w_ff_up, conv_w, conv_b, w_ff_down):
    n = rms_norm(x, g_ffn)
    gate = n @ w_ff_gate
    pad = jnp.pad(gate, ((0, 0), (1, 1), (0, 0)))
    gate = pad[:, :-2] * conv_w[0] + pad[:, 1:-1] * conv_w[1] + pad[:, 2:] * conv_w[2] + conv_b
    return x + (jax.nn.gelu(gate, approximate=False) * (n @ w_ff_up)) @ w_ff_down


def trunk(x, meta_tokens, g_mix, w_in, g_qnorm_a, g_knorm_a, lambda_q1, lambda_k1,
          lambda_q2, lambda_k2, g_subln, w_out, g_ffn, w_ff_gate, w_ff_up, conv_w,
          conv_b, w_ff_down, g_final):
    b, s, d = x.shape
    meta = jnp.broadcast_to(meta_tokens[None].astype(x.dtype), (b, N_META, d))
    h = jnp.concatenate([meta, x], axis=1)
    ang_a = axial_angles(s)
    ang_b = linear_angles(s + N_META)
    for layer in range(DEPTH):
        h = mixer_sublayer(h, ang_a, ang_b, layer, g_mix[layer], w_in[layer],
                           g_qnorm_a[layer], g_knorm_a[layer], lambda_q1[layer],
                           lambda_k1[layer], lambda_q2[layer], lambda_k2[layer],
                           g_subln[layer], w_out[layer])
        h = conv_glu_sublayer(h, g_ffn[layer], w_ff_gate[layer], w_ff_up[layer],
                              conv_w[layer], conv_b[layer], w_ff_down[layer])
    h = rms_norm(h, g_final)
    return h[:, N_META:]


def setup_inputs(seed: int = 0) -> dict:
    key = jax.random.key(seed)
    ks = jax.random.split(key, 20)
    f32 = jnp.float32

    def nrm(k, shape, scale):
        return jax.random.normal(k, shape, f32) * scale

    def gain(k, shape):
        return 1.0 + 0.02 * jax.random.normal(k, shape, f32)

    return {
        "x_prompt": nrm(ks[0], (BATCH, SEQ, D_MODEL), 1.0),
        "x_sample": nrm(ks[1], (DEC_BATCH, DEC_SEQ, D_MODEL), 1.0),
        "meta_tokens": nrm(ks[2], (N_META, D_MODEL), 1.0),
        "g_mix": gain(ks[3], (DEPTH, D_MODEL)),
        "w_in": nrm(ks[4], (DEPTH, D_MODEL, IN_COLS), D_MODEL ** -0.5),
        "g_qnorm_a": gain(ks[5], (DEPTH, HEAD_DIM)),
        "g_knorm_a": gain(ks[6], (DEPTH, HEAD_DIM)),
        "lambda_q1": nrm(ks[7], (DEPTH, HEAD_DIM), 0.1),
        "lambda_k1": nrm(ks[8], (DEPTH, HEAD_DIM), 0.1),
        "lambda_q2": nrm(ks[9], (DEPTH, HEAD_DIM), 0.1),
        "lambda_k2": nrm(ks[10], (DEPTH, HEAD_DIM), 0.1),
        "g_subln": gain(ks[11], (DEPTH, B_V_DIM)),
        "w_out": nrm(ks[12], (DEPTH, MIX_WIDTH, D_MODEL), MIX_WIDTH ** -0.5),
        "g_ffn": gain(ks[13], (DEPTH, D_MODEL)),
        "w_ff_gate": nrm(ks[14], (DEPTH, D_MODEL, D_FF), D_MODEL ** -0.5),
        "w_ff_up": nrm(ks[15], (DEPTH, D_MODEL, D_FF), D_MODEL ** -0.5),
        "conv_w": nrm(ks[16], (DEPTH, CONV_WIDTH, D_FF), CONV_WIDTH ** -0.5),
        "conv_b": nrm(ks[17], (DEPTH, D_FF), 0.01),
        "w_ff_down": nrm(ks[18], (DEPTH, D_FF, D_MODEL), D_FF ** -0.5),
        "g_final": gain(ks[19], (D_MODEL,)),
    }


def reference(x_prompt, x_sample, meta_tokens, g_mix, w_in, g_qnorm_a, g_knorm_a,
              lambda_q1, lambda_k1, lambda_q2, lambda_k2, g_subln, w_out, g_ffn,
              w_ff_gate, w_ff_up, conv_w, conv_b, w_ff_down, g_final):
    y_prompt = trunk(x_prompt, meta_tokens, g_mix, w_in, g_qnorm_a, g_knorm_a,
                     lambda_q1, lambda_k1, lambda_q2, lambda_k2, g_subln, w_out,
                     g_ffn, w_ff_gate, w_ff_up, conv_w, conv_b, w_ff_down, g_final)
    y_sample = trunk(x_sample, meta_tokens, g_mix, w_in, g_qnorm_a, g_knorm_a,
                     lambda_q1, lambda_k1, lambda_q2, lambda_k2, g_subln, w_out,
                     g_ffn, w_ff_gate, w_ff_up, conv_w, conv_b, w_ff_down, g_final)
    return (y_prompt, y_sample)
```

```python
import functools
import math

import jax
import jax.numpy as jnp
import numpy as np
from jax import lax
from jax.experimental import pallas as pl
from jax.experimental.pallas import tpu as pltpu

F32 = jnp.float32
BF16 = jnp.bfloat16

D_MODEL = 1024
HEAD_DIM = 64
N_META = 16
GRID_W = 64
ROPE_THETA = 10000.0
NORM_EPS = 1e-6
LANES = 128
N_PAIRS = 8
N_KV = 5
A_PAIRS = 4
D_FF = 2816
FF_CHUNK = 256
N_FF_CHUNKS = D_FF // FF_CHUNK
IN_COLS = 2304
HALO = 16
META_ROWS = 128
META_Q = 64
LAMBDA_INIT = 0.8 - 0.6 * math.exp(-0.3 * 0)
Q_SCALE = (HEAD_DIM ** -0.5) * math.log2(math.e)
VMEM_LIMIT = 56 * 1024 * 1024


def _params(n_grid_axes):
    return pltpu.CompilerParams(
        dimension_semantics=("arbitrary",) * n_grid_axes,
        vmem_limit_bytes=VMEM_LIMIT,
    )


def _const_spec(shape):
    zeros = (0,) * len(shape)
    return pl.BlockSpec(shape, lambda *_: zeros, pipeline_mode=pl.Buffered(1))


def _proj_kernel(x_ref, gmix_ref, w_ref, gq_ref, gk_ref, bd_ref, ca_ref, sa_ref, cb_ref, sb_ref,
                 q_ref, k_ref, vt_ref):
    x = x_ref[0]
    t = x.shape[0]
    ms = jnp.mean(x * x, axis=-1, keepdims=True)
    n = (x * lax.rsqrt(ms + NORM_EPS) * gmix_ref[...]).astype(BF16)

    lane = lax.broadcasted_iota(jnp.int32, (t, LANES), 1)
    first_half = (lane & (HEAD_DIM // 2)) == 0

    def rope(y, c, s):
        sw = jnp.where(first_half, pltpu.roll(y, LANES - HEAD_DIM // 2, 1),
                       pltpu.roll(y, HEAD_DIM // 2, 1))
        return y * c + sw * s

    def proj(col):
        y = jnp.dot(n, w_ref[:, col:col + 2 * LANES], preferred_element_type=F32)
        return y[:, :LANES], y[:, LANES:]

    def head_ms(y):
        w = y.shape[1]
        return jnp.dot((y * y).astype(BF16), bd_ref[0:w, 0:w], preferred_element_type=F32)

    ca, sa, cb, sb = ca_ref[...], sa_ref[...], cb_ref[...], sb_ref[...]
    gq, gk = gq_ref[...], gk_ref[...]

    for j in range(2):
        y0, y1 = proj(2 * LANES * j)
        r = lax.rsqrt(head_ms(jnp.concatenate([y0, y1], axis=1)) + NORM_EPS) * Q_SCALE
        q_ref[0, 2 * j] = (rope(y0 * gq, ca, sa) * r[:, :LANES]).astype(BF16)
        q_ref[0, 2 * j + 1] = (rope(y1 * gq, ca, sa) * r[:, LANES:]).astype(BF16)
    yk, yv = proj(4 * LANES)
    r = lax.rsqrt(head_ms(yk) + NORM_EPS)
    k_ref[0, 0] = (rope(yk * gk, ca, sa) * r).astype(BF16)
    vt_ref[0, 0] = yv.T.astype(BF16)
    for j in range(2):
        y0, y1 = proj(6 * LANES + 2 * LANES * j)
        q_ref[0, A_PAIRS + 2 * j] = (rope(y0, cb, sb) * Q_SCALE).astype(BF16)
        q_ref[0, A_PAIRS + 2 * j + 1] = (rope(y1, cb, sb) * Q_SCALE).astype(BF16)
        y0, y1 = proj(10 * LANES + 2 * LANES * j)
        k_ref[0, 1 + 2 * j] = rope(y0, cb, sb).astype(BF16)
        k_ref[0, 2 + 2 * j] = rope(y1, cb, sb).astype(BF16)
        y0, y1 = proj(14 * LANES + 2 * LANES * j)
        vt_ref[0, 1 + 2 * j] = y0.T.astype(BF16)
        vt_ref[0, 2 + 2 * j] = y1.T.astype(BF16)


def _proj_call(x, tables, w_in, gmix, gq, gk, bd, tile):
    b, s, d = x.shape
    ca, sa, cb, sb = tables
    grid = (b, s // tile)
    tab_spec = pl.BlockSpec((tile, LANES), lambda bi, i: (i, 0))
    return pl.pallas_call(
        _proj_kernel,
        grid=grid,
        in_specs=[
            pl.BlockSpec((1, tile, d), lambda bi, i: (bi, i, 0)),
            _const_spec((1, d)),
            _const_spec((d, IN_COLS)),
            _const_spec((1, LANES)),
            _const_spec((1, LANES)),
            _const_spec((2 * LANES, 2 * LANES)),
            tab_spec, tab_spec, tab_spec, tab_spec,
        ],
        out_specs=[
            pl.BlockSpec((1, N_PAIRS, tile, LANES), lambda bi, i: (bi, 0, i, 0)),
            pl.BlockSpec((1, N_KV, tile, LANES), lambda bi, i: (bi, 0, i, 0)),
            pl.BlockSpec((1, N_KV, LANES, tile), lambda bi, i: (bi, 0, 0, i)),
        ],
        out_shape=[
            jax.ShapeDtypeStruct((b, N_PAIRS, s, LANES), BF16),
            jax.ShapeDtypeStruct((b, N_KV, s, LANES), BF16),
            jax.ShapeDtypeStruct((b, N_KV, LANES, s), BF16),
        ],
        compiler_params=_params(2),
        name="proj",
    )(x, gmix, w_in, gq, gk, bd, ca, sa, cb, sb)


def _attn_kernel(q_ref, k_ref, vt_ref, km_ref, vtm_ref, lam_ref, gsub_ref, o_ref, s_ref,
                 *, key_chunk):
    tq = q_ref.shape[2]
    s_len = k_ref.shape[2]
    n_chunks = s_len // key_chunk
    lane = lax.broadcasted_iota(jnp.int32, (tq, LANES), 1)
    low = lane < HEAD_DIM
    nt = (((1,), (1,)), ((), ()))

    lam_p = lam_ref[...]
    lam = (jnp.exp(jnp.sum(lam_p[0:1] * lam_p[1:2], axis=-1, keepdims=True))
           - jnp.exp(jnp.sum(lam_p[2:3] * lam_p[3:4], axis=-1, keepdims=True))
           + LAMBDA_INIT)

    def pair_body(p, carry):
        kv = jnp.maximum(p - (A_PAIRS - 1), 0)
        qp = q_ref[0, p].astype(F32)
        qq = jnp.concatenate([jnp.where(low, qp, 0.0), jnp.where(low, 0.0, qp)],
                             axis=0).astype(BF16)

        sm = lax.dot_general(km_ref[0, kv][:N_META], qq, nt, preferred_element_type=F32)
        m8 = jnp.maximum(sm[:8], sm[8:])
        for c in range(n_chunks):
            kc = k_ref[0, kv, c * key_chunk:(c + 1) * key_chunk, :]
            sc = lax.dot_general(kc, qq, nt, preferred_element_type=F32)
            s_ref[c * key_chunk:(c + 1) * key_chunk, :] = sc
            m8 = jnp.maximum(m8, jnp.max(sc.reshape(key_chunk // 8, 8, 2 * tq), axis=0))
        m = jnp.max(m8, axis=0, keepdims=True)

        pm = jnp.exp2(sm - m)
        l8 = pm[:8] + pm[8:]
        pm_pad = jnp.concatenate(
            [pm.astype(BF16), jnp.zeros((META_ROWS - N_META, 2 * tq), BF16)], axis=0)
        acc = jnp.dot(vtm_ref[0, kv], pm_pad, preferred_element_type=F32)
        for c in range(n_chunks):
            pc = jnp.exp2(s_ref[c * key_chunk:(c + 1) * key_chunk, :] - m)
            l8 = l8 + jnp.sum(pc.reshape(key_chunk // 8, 8, 2 * tq), axis=0)
            vc = vt_ref[0, kv, :, c * key_chunk:(c + 1) * key_chunk]
            acc = acc + jnp.dot(vc, pc.astype(BF16), preferred_element_type=F32)
        inv_l = 1.0 / jnp.sum(l8, axis=0, keepdims=True)
        ot = (acc * inv_l).T
        o_lo, o_hi = ot[:tq], ot[tq:]

        @pl.when(p < A_PAIRS)
        def _():
            o_ref[0, p] = jnp.where(low, o_lo, o_hi).astype(BF16)

        @pl.when(p >= A_PAIRS)
        def _():
            dt = o_lo - lam * o_hi
            msd = jnp.mean(dt * dt, axis=-1, keepdims=True)
            o_ref[0, p] = (dt * lax.rsqrt(msd + NORM_EPS) * gsub_ref[...]
                           * (1.0 - LAMBDA_INIT)).astype(BF16)
        return carry

    lax.fori_loop(0, N_PAIRS, pair_body, 0)


def _attn_call(q, k, vt, km, vtm, lam_p, gsub, tq, key_chunk, shared_q):
    bq, _, sq, _ = q.shape
    b, _, s, _ = k.shape
    grid = (b, sq // tq)
    if shared_q:
        q_map = lambda bi, i: (0, 0, i, 0)
    else:
        q_map = lambda bi, i: (bi, 0, i, 0)
    return pl.pallas_call(
        functools.partial(_attn_kernel, key_chunk=key_chunk),
        grid=grid,
        in_specs=[
            pl.BlockSpec((1, N_PAIRS, tq, LANES), q_map),
            pl.BlockSpec((1, N_KV, s, LANES), lambda bi, i: (bi, 0, 0, 0)),
            pl.BlockSpec((1, N_KV, LANES, s), lambda bi, i: (bi, 0, 0, 0)),
            _const_spec((1, N_KV, META_ROWS, LANES)),
            _const_spec((1, N_KV, LANES, META_ROWS)),
            _const_spec((8, LANES)),
            _const_spec((1, LANES)),
        ],
        out_specs=pl.BlockSpec((1, N_PAIRS, tq, LANES), lambda bi, i: (bi, 0, i, 0)),
        out_shape=jax.ShapeDtypeStruct((b, N_PAIRS, sq, LANES), BF16),
        scratch_shapes=[pltpu.VMEM((s, 2 * tq), F32)],
        compiler_params=_params(2),
        name="attn",
    )(q, k, vt, km, vtm, lam_p, gsub)


def _mix_kernel(x_ref, mix_ref, w_ref, g_ref, h_ref, n_ref):
    mix = jnp.concatenate([mix_ref[0, p] for p in range(N_PAIRS)], axis=1)
    h = x_ref[0] + jnp.dot(mix, w_ref[...], preferred_element_type=F32)
    h_ref[0] = h
    ms = jnp.mean(h * h, axis=-1, keepdims=True)
    n_ref[0] = (h * lax.rsqrt(ms + NORM_EPS) * g_ref[...]).astype(BF16)


def _mix_call(x, mix, w_out, g_ffn, tile, shared_x):
    b, _, s, _ = mix.shape
    d = x.shape[-1]
    grid = (b, s // tile)
    if shared_x:
        x_map = lambda bi, i: (0, i, 0)
    else:
        x_map = lambda bi, i: (bi, i, 0)
    return pl.pallas_call(
        _mix_kernel,
        grid=grid,
        in_specs=[
            pl.BlockSpec((1, tile, d), x_map),
            pl.BlockSpec((1, N_PAIRS, tile, LANES), lambda bi, i: (bi, 0, i, 0)),
            _const_spec((d, d)),
            _const_spec((1, d)),
        ],
        out_specs=[
            pl.BlockSpec((1, tile, d), lambda bi, i: (bi, i, 0)),
            pl.BlockSpec((1, tile, d), lambda bi, i: (bi, i, 0)),
        ],
        out_shape=[
            jax.ShapeDtypeStruct((b, s, d), F32),
            jax.ShapeDtypeStruct((b, s, d), BF16),
        ],
        compiler_params=_params(2),
        name="mix",
    )(x, mix, w_out, g_ffn)


def _ffn_kernel(h_ref, n_ref, left_ref, meta_ref, right_ref, wg_ref, wu_ref, cw_ref, wd_ref,
                gfin_ref, o_ref, gate_ref, acc_ref):
    i = pl.program_id(1)
    last = pl.num_programs(1) - 1
    t = n_ref.shape[1]
    n = n_ref[0]
    left = jnp.where(i == 0, meta_ref[0], left_ref[0])
    right = jnp.where(i == last, jnp.zeros_like(right_ref[0]), right_ref[0])
    n_ext = jnp.concatenate([left, n, right], axis=0)

    for c in range(N_FF_CHUNKS):
        gate_ref[...] = jnp.dot(n_ext, wg_ref[c], preferred_element_type=F32)
        cw = cw_ref[c]
        g = (gate_ref[HALO - 1:HALO - 1 + t, :] * cw[0:1]
             + gate_ref[HALO:HALO + t, :] * cw[1:2]
             + gate_ref[HALO + 1:HALO + 1 + t, :] * cw[2:3]
             + cw[3:4])
        up = jnp.dot(n, wu_ref[c], preferred_element_type=F32)
        act = 0.5 * g * (1.0 + lax.erf(g * (2.0 ** -0.5)))
        u = (act * up).astype(BF16)
        part = jnp.dot(u, wd_ref[c], preferred_element_type=F32)
        if c == 0:
            acc_ref[...] = h_ref[0] + part
        else:
            acc_ref[...] += part
    y = acc_ref[...]
    ms = jnp.mean(y * y, axis=-1, keepdims=True)
    o_ref[0] = y * lax.rsqrt(ms + NORM_EPS) * gfin_ref[...]


def _ffn_call(h, n2, n2_meta, wg, wu, cw, wd, g_final, tile):
    b, s, d = h.shape
    grid = (b, s // tile)
    per = tile // HALO
    n_halo_blocks = s // HALO
    return pl.pallas_call(
        _ffn_kernel,
        grid=grid,
        in_specs=[
            pl.BlockSpec((1, tile, d), lambda bi, i: (bi, i, 0)),
            pl.BlockSpec((1, tile, d), lambda bi, i: (bi, i, 0)),
            pl.BlockSpec((1, HALO, d), lambda bi, i: (bi, jnp.maximum(i * per - 1, 0), 0)),
            pl.BlockSpec((1, HALO, d), lambda bi, i: (bi, 0, 0)),
            pl.BlockSpec((1, HALO, d),
                         lambda bi, i: (bi, jnp.minimum((i + 1) * per, n_halo_blocks - 1), 0)),
            _const_spec((N_FF_CHUNKS, d, FF_CHUNK)),
            _const_spec((N_FF_CHUNKS, d, FF_CHUNK)),
            _const_spec((N_FF_CHUNKS, 8, FF_CHUNK)),
            _const_spec((N_FF_CHUNKS, FF_CHUNK, d)),
            _const_spec((1, d)),
        ],
        out_specs=pl.BlockSpec((1, tile, d), lambda bi, i: (bi, i, 0)),
        out_shape=jax.ShapeDtypeStruct((b, s, d), F32),
        scratch_shapes=[
            pltpu.VMEM((tile + 2 * HALO, FF_CHUNK), F32),
            pltpu.VMEM((tile, d), F32),
        ],
        compiler_params=_params(2),
        name="ffn",
    )(h, n2, n2, n2_meta, n2, wg, wu, cw, wd, g_final)


def _pair_layout(ang):
    c = jnp.cos(ang)
    s = jnp.sin(ang)
    return jnp.tile(c, (1, 4)), jnp.tile(jnp.concatenate([-s, s], axis=-1), (1, 2))


def _linear_inv():
    return ROPE_THETA ** (-jnp.arange(0, HEAD_DIM, 2, dtype=F32) / HEAD_DIM)


def _real_tables(s):
    t = jnp.arange(s)
    rowp = (t // GRID_W).astype(F32)
    colp = (t % GRID_W).astype(F32)
    axis_dim = HEAD_DIM // 2
    inv_a = ROPE_THETA ** (-jnp.arange(0, axis_dim, 2, dtype=F32) / axis_dim)
    ang_a = jnp.concatenate([rowp[:, None] * inv_a[None], colp[:, None] * inv_a[None]], axis=-1)
    pos = jnp.arange(N_META + s, dtype=F32)[N_META:]
    ang_b = pos[:, None] * _linear_inv()[None]
    return _pair_layout(ang_a) + _pair_layout(ang_b)


def _meta_tables():
    ang_a = jnp.zeros((META_ROWS, HEAD_DIM // 2), F32)
    pos = jnp.arange(META_ROWS, dtype=F32)
    ang_b = pos[:, None] * _linear_inv()[None]
    return _pair_layout(ang_a) + _pair_layout(ang_b)


def _trunk(x, meta_x, meta_qkv, prm, cfg):
    s = x.shape[1]
    q, k, vt = _proj_call(x, _real_tables(s), prm["w_in"], prm["g_mix"], prm["gq"], prm["gk"],
                          prm["bd"], min(s, cfg["proj_tile"]))
    q_m, k_m, vt_m = meta_qkv
    kc = min(s, cfg["key_chunk"])
    mix = _attn_call(q, k, vt, k_m, vt_m, prm["lam"], prm["g_subln"],
                     min(s, cfg["tq"]), kc, shared_q=False)
    mix_m = _attn_call(q_m[:, :, :META_Q], k, vt, k_m, vt_m, prm["lam"], prm["g_subln"],
                       META_Q, kc, shared_q=True)
    row_tile = min(s, cfg["row_tile"])
    h1, n2 = _mix_call(x, mix, prm["w_out"], prm["g_ffn"], row_tile, False)
    _, n2_m = _mix_call(meta_x[:, :META_Q], mix_m, prm["w_out"], prm["g_ffn"], META_Q, True)
    return _ffn_call(h1, n2, n2_m, prm["wg"], prm["wu"], prm["cw"], prm["wd"], prm["g_final"],
                     row_tile)


_CFG = dict(proj_tile=512, tq=256, key_chunk=512, row_tile=512)


def kernel(x_prompt, x_sample, meta_tokens, g_mix, w_in, g_qnorm_a, g_knorm_a, lambda_q1, lambda_k1,
           lambda_q2, lambda_k2, g_subln, w_out, g_ffn, w_ff_gate, w_ff_up, conv_w, conv_b,
           w_ff_down, g_final):
    assert w_in.shape[0] == 1, "single-layer trunk"
    d = D_MODEL
    head_order = np.array([0, 4, 1, 5, 2, 6, 3, 7])
    a_cols = (head_order[:, None] * HEAD_DIM + np.arange(HEAD_DIM)[None]).reshape(-1)
    in_perm = np.concatenate([a_cols, np.arange(A_PAIRS * LANES, IN_COLS)])
    out_perm = np.concatenate([a_cols, np.arange(A_PAIRS * LANES, d)])
    blk = np.arange(2 * LANES) // HEAD_DIM
    cw = jnp.concatenate([conv_w[0], conv_b[0][None], jnp.zeros((4, D_FF), F32)], axis=0)
    lam = jnp.stack([lambda_q1[0], lambda_k1[0], lambda_q2[0], lambda_k2[0]])
    prm = dict(
        w_in=w_in[0][:, in_perm].astype(BF16),
        g_mix=g_mix[0][None],
        gq=jnp.tile(g_qnorm_a[0], 2)[None],
        gk=jnp.tile(g_knorm_a[0], 2)[None],
        bd=jnp.asarray((blk[:, None] == blk[None]) / HEAD_DIM, BF16),
        lam=jnp.zeros((8, LANES), F32).at[:4, :HEAD_DIM].set(lam),
        g_subln=g_subln[0][None],
        w_out=w_out[0][out_perm].astype(BF16),
        g_ffn=g_ffn[0][None],
        wg=w_ff_gate[0].reshape(d, N_FF_CHUNKS, FF_CHUNK).transpose(1, 0, 2).astype(BF16),
        wu=w_ff_up[0].reshape(d, N_FF_CHUNKS, FF_CHUNK).transpose(1, 0, 2).astype(BF16),
        cw=cw.reshape(8, N_FF_CHUNKS, FF_CHUNK).transpose(1, 0, 2),
        wd=w_ff_down[0].reshape(N_FF_CHUNKS, FF_CHUNK, d).astype(BF16),
        g_final=g_final[None],
    )
    meta_x = jnp.zeros((1, META_ROWS, d), F32).at[0, :N_META].set(meta_tokens)
    meta_qkv = _proj_call(meta_x, _meta_tables(), prm["w_in"], prm["g_mix"], prm["gq"], prm["gk"],
                          prm["bd"], META_ROWS)
    y_prompt = _trunk(x_prompt, meta_x, meta_qkv, prm, _CFG)
    y_sample = _trunk(x_sample, meta_x, meta_qkv, prm, _CFG)
    return (y_prompt, y_sample)
```

```python
import functools
import math

import jax
import jax.numpy as jnp
import numpy as np
from jax import lax
from jax.experimental import pallas as pl
from jax.experimental.pallas import tpu as pltpu

F32 = jnp.float32
BF16 = jnp.bfloat16

D_MODEL = 1024
HEAD_DIM = 64
N_META = 16
GRID_W = 64
ROPE_THETA = 10000.0
NORM_EPS = 1e-6
LANES = 128
N_PAIRS = 8
N_KV = 5
A_PAIRS = 4
D_FF = 2816
FF_CHUNK = 256
N_FF_CHUNKS = D_FF // FF_CHUNK
IN_COLS = 2304
HALO = 16
META_ROWS = 128
META_Q = 64
LAMBDA_INIT = 0.8 - 0.6 * math.exp(-0.3 * 0)
Q_SCALE = (HEAD_DIM ** -0.5) * math.log2(math.e)
VMEM_LIMIT = 56 * 1024 * 1024


def _params(n_grid_axes):
    return pltpu.CompilerParams(
        dimension_semantics=("arbitrary",) * n_grid_axes,
        vmem_limit_bytes=VMEM_LIMIT,
    )


def _const_spec(shape):
    zeros = (0,) * len(shape)
    return pl.BlockSpec(shape, lambda *_: zeros, pipeline_mode=pl.Buffered(1))


def _proj_kernel(x_ref, gmix_ref, w_ref, gq_ref, gk_ref, bd_ref, ca_ref, sa_ref, cb_ref, sb_ref,
                 q_ref, k_ref, vt_ref):
    x = x_ref[0]
    t = x.shape[0]
    ms = jnp.mean(x * x, axis=-1, keepdims=True)
    n = (x * lax.rsqrt(ms + NORM_EPS) * gmix_ref[...]).astype(BF16)

    lane = lax.broadcasted_iota(jnp.int32, (t, LANES), 1)
    first_half = (lane & (HEAD_DIM // 2)) == 0

    def rope(y, c, s):
        sw = jnp.where(first_half, pltpu.roll(y, LANES - HEAD_DIM // 2, 1),
                       pltpu.roll(y, HEAD_DIM // 2, 1))
        return y * c + sw * s

    def proj(col):
        y = jnp.dot(n, w_ref[:, col:col + 2 * LANES], preferred_element_type=F32)
        return y[:, :LANES], y[:, LANES:]

    def head_ms(y):
        w = y.shape[1]
        return jnp.dot((y * y).astype(BF16), bd_ref[0:w, 0:w], preferred_element_type=F32)

    ca, sa, cb, sb = ca_ref[...], sa_ref[...], cb_ref[...], sb_ref[...]
    gq, gk = gq_ref[...], gk_ref[...]

    for j in range(2):
        y0, y1 = proj(2 * LANES * j)
        r = lax.rsqrt(head_ms(jnp.concatenate([y0, y1], axis=1)) + NORM_EPS) * Q_SCALE
        q_ref[0, 2 * j] = (rope(y0 * gq, ca, sa) * r[:, :LANES]).astype(BF16)
        q_ref[0, 2 * j + 1] = (rope(y1 * gq, ca, sa) * r[:, LANES:]).astype(BF16)
    yk, yv = proj(4 * LANES)
    r = lax.rsqrt(head_ms(yk) + NORM_EPS)
    k_ref[0, 0] = (rope(yk * gk, ca, sa) * r).astype(BF16)
    vt_ref[0, 0] = yv.T.astype(BF16)
    for j in range(2):
        y0, y1 = proj(6 * LANES + 2 * LANES * j)
        q_ref[0, A_PAIRS + 2 * j] = (rope(y0, cb, sb) * Q_SCALE).astype(BF16)
        q_ref[0, A_PAIRS + 2 * j + 1] = (rope(y1, cb, sb) * Q_SCALE).astype(BF16)
        y0, y1 = proj(10 * LANES + 2 * LANES * j)
        k_ref[0, 1 + 2 * j] = rope(y0, cb, sb).astype(BF16)
        k_ref[0, 2 + 2 * j] = rope(y1, cb, sb).astype(BF16)
        y0, y1 = proj(14 * LANES + 2 * LANES * j)
        vt_ref[0, 1 + 2 * j] = y0.T.astype(BF16)
        vt_ref[0, 2 + 2 * j] = y1.T.astype(BF16)


def _proj_call(x, tables, w_in, gmix, gq, gk, bd, tile):
    b, s, d = x.shape
    ca, sa, cb, sb = tables
    grid = (b, s // tile)
    tab_spec = pl.BlockSpec((tile, LANES), lambda bi, i: (i, 0))
    return pl.pallas_call(
        _proj_kernel,
        grid=grid,
        in_specs=[
            pl.BlockSpec((1, tile, d), lambda bi, i: (bi, i, 0)),
            _const_spec((1, d)),
            _const_spec((d, IN_COLS)),
            _const_spec((1, LANES)),
            _const_spec((1, LANES)),
            _const_spec((2 * LANES, 2 * LANES)),
            tab_spec, tab_spec, tab_spec, tab_spec,
        ],
        out_specs=[
            pl.BlockSpec((1, N_PAIRS, tile, LANES), lambda bi, i: (bi, 0, i, 0)),
            pl.BlockSpec((1, N_KV, tile, LANES), lambda bi, i: (bi, 0, i, 0)),
            pl.BlockSpec((1, N_KV, LANES, tile), lambda bi, i: (bi, 0, 0, i)),
        ],
        out_shape=[
            jax.ShapeDtypeStruct((b, N_PAIRS, s, LANES), BF16),
            jax.ShapeDtypeStruct((b, N_KV, s, LANES), BF16),
            jax.ShapeDtypeStruct((b, N_KV, LANES, s), BF16),
        ],
        compiler_params=_params(2),
        name="proj",
    )(x, gmix, w_in, gq, gk, bd, ca, sa, cb, sb)


def _attn_kernel(q_ref, k_ref, vt_ref, km_ref, vtm_ref, lam_ref, gsub_ref, o_ref, sa_ref, sb_ref,
                 *, key_chunk):
    tq = q_ref.shape[2]
    s_len = k_ref.shape[2]
    n_chunks = s_len // key_chunk
    lane = lax.broadcasted_iota(jnp.int32, (tq, LANES), 1)
    low = lane < HEAD_DIM
    nt = (((1,), (1,)), ((), ()))

    lam_p = lam_ref[...]
    lam = (jnp.exp(jnp.sum(lam_p[0:1] * lam_p[1:2], axis=-1, keepdims=True))
           - jnp.exp(jnp.sum(lam_p[2:3] * lam_p[3:4], axis=-1, keepdims=True))
           + LAMBDA_INIT)

    def chunk(c):
        return slice(c * key_chunk, (c + 1) * key_chunk)

    def stage(scores_job, softmax_job):
        if scores_job is not None:
            b1, j1, buf1 = scores_job
            p1 = j1 + A_PAIRS if b1 else j1
            kv1 = j1 + 1 if b1 else 0
            qp = q_ref[0, p1].astype(F32)
            qq = jnp.concatenate([jnp.where(low, qp, 0.0), jnp.where(low, 0.0, qp)],
                                 axis=0).astype(BF16)
            sm1 = lax.dot_general(km_ref[0, kv1][:N_META], qq, nt, preferred_element_type=F32)
            m8 = jnp.maximum(sm1[:8], sm1[8:])
        if softmax_job is not None:
            b2, j2, buf2, (sm2, m2) = softmax_job
            p2 = j2 + A_PAIRS if b2 else j2
            kv2 = j2 + 1 if b2 else 0
            pm = jnp.exp2(sm2 - m2)
            l8 = pm[:8] + pm[8:]
            pm_pad = jnp.concatenate(
                [pm.astype(BF16), jnp.zeros((META_ROWS - N_META, 2 * tq), BF16)], axis=0)
            acc = jnp.dot(vtm_ref[0, kv2], pm_pad, preferred_element_type=F32)
        for c in range(n_chunks):
            if scores_job is not None:
                sc = lax.dot_general(k_ref[0, kv1, chunk(c), :], qq, nt,
                                     preferred_element_type=F32)
                buf1[chunk(c), :] = sc
                m8 = jnp.maximum(m8, jnp.max(sc.reshape(key_chunk // 8, 8, 2 * tq), axis=0))
            if softmax_job is not None:
                pc = jnp.exp2(buf2[chunk(c), :] - m2)
                l8 = l8 + jnp.sum(pc.reshape(key_chunk // 8, 8, 2 * tq), axis=0)
                acc = acc + jnp.dot(vt_ref[0, kv2, :, chunk(c)], pc.astype(BF16),
                                    preferred_element_type=F32)
        if softmax_job is not None:
            inv_l = 1.0 / jnp.sum(l8, axis=0, keepdims=True)
            ot = (acc * inv_l).T
            o_lo, o_hi = ot[:tq], ot[tq:]
            if b2:
                dt = o_lo - lam * o_hi
                msd = jnp.mean(dt * dt, axis=-1, keepdims=True)
                o_ref[0, p2] = (dt * lax.rsqrt(msd + NORM_EPS) * gsub_ref[...]
                                * (1.0 - LAMBDA_INIT)).astype(BF16)
            else:
                o_ref[0, p2] = jnp.where(low, o_lo, o_hi).astype(BF16)
        if scores_job is not None:
            return sm1, jnp.max(m8, axis=0, keepdims=True)
        return None

    def body(j, stats_a):
        stats_b = stage((True, j, sb_ref), (False, j, sa_ref, stats_a))
        return stage((False, j + 1, sa_ref), (True, j, sb_ref, stats_b))

    last = A_PAIRS - 1
    stats_a = stage((False, 0, sa_ref), None)
    stats_a = lax.fori_loop(0, last, body, stats_a)
    stats_b = stage((True, last, sb_ref), (False, last, sa_ref, stats_a))
    stage(None, (True, last, sb_ref, stats_b))


def _attn_call(q, k, vt, km, vtm, lam_p, gsub, tq, key_chunk, shared_q):
    bq, _, sq, _ = q.shape
    b, _, s, _ = k.shape
    grid = (b, sq // tq)
    if shared_q:
        q_map = lambda bi, i: (0, 0, i, 0)
    else:
        q_map = lambda bi, i: (bi, 0, i, 0)
    return pl.pallas_call(
        functools.partial(_attn_kernel, key_chunk=key_chunk),
        grid=grid,
        in_specs=[
            pl.BlockSpec((1, N_PAIRS, tq, LANES), q_map),
            pl.BlockSpec((1, N_KV, s, LANES), lambda bi, i: (bi, 0, 0, 0)),
            pl.BlockSpec((1, N_KV, LANES, s), lambda bi, i: (bi, 0, 0, 0)),
            _const_spec((1, N_KV, META_ROWS, LANES)),
            _const_spec((1, N_KV, LANES, META_ROWS)),
            _const_spec((8, LANES)),
            _const_spec((1, LANES)),
        ],
        out_specs=pl.BlockSpec((1, N_PAIRS, tq, LANES), lambda bi, i: (bi, 0, i, 0)),
        out_shape=jax.ShapeDtypeStruct((b, N_PAIRS, sq, LANES), BF16),
        scratch_shapes=[pltpu.VMEM((s, 2 * tq), F32), pltpu.VMEM((s, 2 * tq), F32)],
        compiler_params=_params(2),
        name="attn",
    )(q, k, vt, km, vtm, lam_p, gsub)


def _mix_kernel(x_ref, mix_ref, w_ref, g_ref, h_ref, n_ref):
    mix = jnp.concatenate([mix_ref[0, p] for p in range(N_PAIRS)], axis=1)
    h = x_ref[0] + jnp.dot(mix, w_ref[...], preferred_element_type=F32)
    h_ref[0] = h
    ms = jnp.mean(h * h, axis=-1, keepdims=True)
    n_ref[0] = (h * lax.rsqrt(ms + NORM_EPS) * g_ref[...]).astype(BF16)


def _mix_call(x, mix, w_out, g_ffn, tile, shared_x):
    b, _, s, _ = mix.shape
    d = x.shape[-1]
    grid = (b, s // tile)
    if shared_x:
        x_map = lambda bi, i: (0, i, 0)
    else:
        x_map = lambda bi, i: (bi, i, 0)
    return pl.pallas_call(
        _mix_kernel,
        grid=grid,
        in_specs=[
            pl.BlockSpec((1, tile, d), x_map),
            pl.BlockSpec((1, N_PAIRS, tile, LANES), lambda bi, i: (bi, 0, i, 0)),
            _const_spec((d, d)),
            _const_spec((1, d)),
        ],
        out_specs=[
            pl.BlockSpec((1, tile, d), lambda bi, i: (bi, i, 0)),
            pl.BlockSpec((1, tile, d), lambda bi, i: (bi, i, 0)),
        ],
        out_shape=[
            jax.ShapeDtypeStruct((b, s, d), F32),
            jax.ShapeDtypeStruct((b, s, d), BF16),
        ],
        compiler_params=_params(2),
        name="mix",
    )(x, mix, w_out, g_ffn)


def _ffn_kernel(h_ref, n_ref, left_ref, meta_ref, right_ref, wg_ref, wu_ref, cw_ref, wd_ref,
                gfin_ref, o_ref, gate0_ref, gate1_ref, u_ref):
    i = pl.program_id(1)
    last = pl.num_programs(1) - 1
    t = n_ref.shape[1]
    n = n_ref[0]
    left = jnp.where(i == 0, meta_ref[0], left_ref[0])
    right = jnp.where(i == last, jnp.zeros_like(right_ref[0]), right_ref[0])
    n_ext = jnp.concatenate([left, n, right], axis=0)

    for c in range(N_FF_CHUNKS):
        gate_ref = gate1_ref if c % 2 else gate0_ref
        gate_ref[...] = jnp.dot(n_ext, wg_ref[c], preferred_element_type=F32)
        cw = cw_ref[c]
        g = (gate_ref[HALO - 1:HALO - 1 + t, :] * cw[0:1]
             + gate_ref[HALO:HALO + t, :] * cw[1:2]
             + gate_ref[HALO + 1:HALO + 1 + t, :] * cw[2:3]
             + cw[3:4])
        up = jnp.dot(n, wu_ref[c], preferred_element_type=F32)
        act = 0.5 * g * (1.0 + lax.erf(g * (2.0 ** -0.5)))
        u_ref[:, c * FF_CHUNK:(c + 1) * FF_CHUNK] = (act * up).astype(BF16)
    y = h_ref[0] + jnp.dot(u_ref[...], wd_ref[...], preferred_element_type=F32)
    ms = jnp.mean(y * y, axis=-1, keepdims=True)
    o_ref[0] = y * lax.rsqrt(ms + NORM_EPS) * gfin_ref[...]


def _ffn_call(h, n2, n2_meta, wg, wu, cw, wd, g_final, tile):
    b, s, d = h.shape
    grid = (b, s // tile)
    per = tile // HALO
    n_halo_blocks = s // HALO
    return pl.pallas_call(
        _ffn_kernel,
        grid=grid,
        in_specs=[
            pl.BlockSpec((1, tile, d), lambda bi, i: (bi, i, 0)),
            pl.BlockSpec((1, tile, d), lambda bi, i: (bi, i, 0)),
            pl.BlockSpec((1, HALO, d), lambda bi, i: (bi, jnp.maximum(i * per - 1, 0), 0)),
            pl.BlockSpec((1, HALO, d), lambda bi, i: (bi, 0, 0)),
            pl.BlockSpec((1, HALO, d),
                         lambda bi, i: (bi, jnp.minimum((i + 1) * per, n_halo_blocks - 1), 0)),
            _const_spec((N_FF_CHUNKS, d, FF_CHUNK)),
            _const_spec((N_FF_CHUNKS, d, FF_CHUNK)),
            _const_spec((N_FF_CHUNKS, 8, FF_CHUNK)),
            _const_spec((D_FF, d)),
            _const_spec((1, d)),
        ],
        out_specs=pl.BlockSpec((1, tile, d), lambda bi, i: (bi, i, 0)),
        out_shape=jax.ShapeDtypeStruct((b, s, d), F32),
        scratch_shapes=[
            pltpu.VMEM((tile + 2 * HALO, FF_CHUNK), F32),
            pltpu.VMEM((tile + 2 * HALO, FF_CHUNK), F32),
            pltpu.VMEM((tile, D_FF), BF16),
        ],
        compiler_params=_params(2),
        name="ffn",
    )(h, n2, n2, n2_meta, n2, wg, wu, cw, wd, g_final)


def _pair_layout(ang):
    c = jnp.cos(ang)
    s = jnp.sin(ang)
    return jnp.tile(c, (1, 4)), jnp.tile(jnp.concatenate([-s, s], axis=-1), (1, 2))


def _linear_inv():
    return ROPE_THETA ** (-jnp.arange(0, HEAD_DIM, 2, dtype=F32) / HEAD_DIM)


def _real_tables(s):
    t = jnp.arange(s)
    rowp = (t // GRID_W).astype(F32)
    colp = (t % GRID_W).astype(F32)
    axis_dim = HEAD_DIM // 2
    inv_a = ROPE_THETA ** (-jnp.arange(0, axis_dim, 2, dtype=F32) / axis_dim)
    ang_a = jnp.concatenate([rowp[:, None] * inv_a[None], colp[:, None] * inv_a[None]], axis=-1)
    pos = jnp.arange(N_META + s, dtype=F32)[N_META:]
    ang_b = pos[:, None] * _linear_inv()[None]
    return _pair_layout(ang_a) + _pair_layout(ang_b)


def _meta_tables():
    ang_a = jnp.zeros((META_ROWS, HEAD_DIM // 2), F32)
    pos = jnp.arange(META_ROWS, dtype=F32)
    ang_b = pos[:, None] * _linear_inv()[None]
    return _pair_layout(ang_a) + _pair_layout(ang_b)


def _trunk(x, meta_x, meta_qkv, prm, cfg):
    s = x.shape[1]
    q, k, vt = _proj_call(x, _real_tables(s), prm["w_in"], prm["g_mix"], prm["gq"], prm["gk"],
                          prm["bd"], min(s, cfg["proj_tile"]))
    q_m, k_m, vt_m = meta_qkv
    kc = min(s, cfg["key_chunk"])
    mix = _attn_call(q, k, vt, k_m, vt_m, prm["lam"], prm["g_subln"],
                     min(s, cfg["tq"]), kc, shared_q=False)
    mix_m = _attn_call(q_m[:, :, :META_Q], k, vt, k_m, vt_m, prm["lam"], prm["g_subln"],
                       META_Q, kc, shared_q=True)
    row_tile = min(s, cfg["row_tile"])
    h1, n2 = _mix_call(x, mix, prm["w_out"], prm["g_ffn"], row_tile, False)
    _, n2_m = _mix_call(meta_x[:, :META_Q], mix_m, prm["w_out"], prm["g_ffn"], META_Q, True)
    return _ffn_call(h1, n2, n2_m, prm["wg"], prm["wu"], prm["cw"], prm["wd"], prm["g_final"],
                     row_tile)


_CFG = dict(proj_tile=512, tq=256, key_chunk=512, row_tile=512)


def kernel(x_prompt, x_sample, meta_tokens, g_mix, w_in, g_qnorm_a, g_knorm_a, lambda_q1, lambda_k1,
           lambda_q2, lambda_k2, g_subln, w_out, g_ffn, w_ff_gate, w_ff_up, conv_w, conv_b,
           w_ff_down, g_final):
    assert w_in.shape[0] == 1, "single-layer trunk"
    d = D_MODEL
    head_order = np.array([0, 4, 1, 5, 2, 6, 3, 7])
    a_cols = (head_order[:, None] * HEAD_DIM + np.arange(HEAD_DIM)[None]).reshape(-1)
    in_perm = np.concatenate([a_cols, np.arange(A_PAIRS * LANES, IN_COLS)])
    out_perm = np.concatenate([a_cols, np.arange(A_PAIRS * LANES, d)])
    blk = np.arange(2 * LANES) // HEAD_DIM
    cw = jnp.concatenate([conv_w[0], conv_b[0][None], jnp.zeros((4, D_FF), F32)], axis=0)
    lam = jnp.stack([lambda_q1[0], lambda_k1[0], lambda_q2[0], lambda_k2[0]])
    prm = dict(
        w_in=w_in[0][:, in_perm].astype(BF16),
        g_mix=g_mix[0][None],
        gq=jnp.tile(g_qnorm_a[0], 2)[None],
        gk=jnp.tile(g_knorm_a[0], 2)[None],
        bd=jnp.asarray((blk[:, None] == blk[None]) / HEAD_DIM, BF16),
        lam=jnp.zeros((8, LANES), F32).at[:4, :HEAD_DIM].set(lam),
        g_subln=g_subln[0][None],
        w_out=w_out[0][out_perm].astype(BF16),
        g_ffn=g_ffn[0][None],
        wg=w_ff_gate[0].reshape(d, N_FF_CHUNKS, FF_CHUNK).transpose(1, 0, 2).astype(BF16),
        wu=w_ff_up[0].reshape(d, N_FF_CHUNKS, FF_CHUNK).transpose(1, 0, 2).astype(BF16),
        cw=cw.reshape(8, N_FF_CHUNKS, FF_CHUNK).transpose(1, 0, 2),
        wd=w_ff_down[0].astype(BF16),
        g_final=g_final[None],
    )
    meta_x = jnp.zeros((1, META_ROWS, d), F32).at[0, :N_META].set(meta_tokens)
    meta_qkv = _proj_call(meta_x, _meta_tables(), prm["w_in"], prm["g_mix"], prm["gq"], prm["gk"],
                          prm["bd"], META_ROWS)
    y_prompt = _trunk(x_prompt, meta_x, meta_qkv, prm, _CFG)
    y_sample = _trunk(x_sample, meta_x, meta_qkv, prm, _CFG)
    return (y_prompt, y_sample)
```

```python
import functools
import math

import jax
import jax.numpy as jnp
import numpy as np
from jax import lax
from jax.experimental import pallas as pl
from jax.experimental.pallas import tpu as pltpu

F32 = jnp.float32
BF16 = jnp.bfloat16

D_MODEL = 1024
HEAD_DIM = 64
N_META = 16
GRID_W = 64
ROPE_THETA = 10000.0
NORM_EPS = 1e-6
LANES = 128
N_PAIRS = 8
N_KV = 5
A_PAIRS = 4
D_FF = 2816
FF_CHUNK = 256
N_FF_CHUNKS = D_FF // FF_CHUNK
IN_COLS = 2304
HALO = 16
META_ROWS = 128
META_Q = 64
LAMBDA_INIT = 0.8 - 0.6 * math.exp(-0.3 * 0)
Q_SCALE = (HEAD_DIM ** -0.5) * math.log2(math.e)
VMEM_LIMIT = 56 * 1024 * 1024


def _params(n_grid_axes):
    return pltpu.CompilerParams(
        dimension_semantics=("arbitrary",) * n_grid_axes,
        vmem_limit_bytes=VMEM_LIMIT,
    )


def _const_spec(shape):
    zeros = (0,) * len(shape)
    return pl.BlockSpec(shape, lambda *_: zeros, pipeline_mode=pl.Buffered(1))


def _proj_kernel(x_ref, gmix_ref, w_ref, gq_ref, gk_ref, bd_ref, ca_ref, sa_ref, cb_ref, sb_ref,
                 q_ref, k_ref, vt_ref):
    x = x_ref[0]
    t = x.shape[0]
    ms = jnp.mean(x * x, axis=-1, keepdims=True)
    n = (x * lax.rsqrt(ms + NORM_EPS) * gmix_ref[...]).astype(BF16)

    lane = lax.broadcasted_iota(jnp.int32, (t, LANES), 1)
    first_half = (lane & (HEAD_DIM // 2)) == 0

    def rope(y, c, s):
        sw = jnp.where(first_half, pltpu.roll(y, LANES - HEAD_DIM // 2, 1),
                       pltpu.roll(y, HEAD_DIM // 2, 1))
        return y * c + sw * s

    def proj(col):
        y = jnp.dot(n, w_ref[:, col:col + 2 * LANES], preferred_element_type=F32)
        return y[:, :LANES], y[:, LANES:]

    def head_ms(y):
        w = y.shape[1]
        return jnp.dot((y * y).astype(BF16), bd_ref[0:w, 0:w], preferred_element_type=F32)

    ca, sa, cb, sb = ca_ref[...], sa_ref[...], cb_ref[...], sb_ref[...]
    gq, gk = gq_ref[...], gk_ref[...]

    for j in range(2):
        y0, y1 = proj(2 * LANES * j)
        r = lax.rsqrt(head_ms(jnp.concatenate([y0, y1], axis=1)) + NORM_EPS) * Q_SCALE
        q_ref[0, 2 * j] = (rope(y0 * gq, ca, sa) * r[:, :LANES]).astype(BF16)
        q_ref[0, 2 * j + 1] = (rope(y1 * gq, ca, sa) * r[:, LANES:]).astype(BF16)
    yk, yv = proj(4 * LANES)
    r = lax.rsqrt(head_ms(yk) + NORM_EPS)
    k_ref[0, 0] = (rope(yk * gk, ca, sa) * r).astype(BF16)
    vt_ref[0, 0] = yv.T.astype(BF16)
    for j in range(2):
        y0, y1 = proj(6 * LANES + 2 * LANES * j)
        q_ref[0, A_PAIRS + 2 * j] = (rope(y0, cb, sb) * Q_SCALE).astype(BF16)
        q_ref[0, A_PAIRS + 2 * j + 1] = (rope(y1, cb, sb) * Q_SCALE).astype(BF16)
        y0, y1 = proj(10 * LANES + 2 * LANES * j)
        k_ref[0, 1 + 2 * j] = rope(y0, cb, sb).astype(BF16)
        k_ref[0, 2 + 2 * j] = rope(y1, cb, sb).astype(BF16)
        y0, y1 = proj(14 * LANES + 2 * LANES * j)
        vt_ref[0, 1 + 2 * j] = y0.T.astype(BF16)
        vt_ref[0, 2 + 2 * j] = y1.T.astype(BF16)


def _proj_call(x, tables, w_in, gmix, gq, gk, bd, tile):
    b, s, d = x.shape
    ca, sa, cb, sb = tables
    grid = (b, s // tile)
    tab_spec = pl.BlockSpec((tile, LANES), lambda bi, i: (i, 0))
    return pl.pallas_call(
        _proj_kernel,
        grid=grid,
        in_specs=[
            pl.BlockSpec((1, tile, d), lambda bi, i: (bi, i, 0)),
            _const_spec((1, d)),
            _const_spec((d, IN_COLS)),
            _const_spec((1, LANES)),
            _const_spec((1, LANES)),
            _const_spec((2 * LANES, 2 * LANES)),
            tab_spec, tab_spec, tab_spec, tab_spec,
        ],
        out_specs=[
            pl.BlockSpec((1, N_PAIRS, tile, LANES), lambda bi, i: (bi, 0, i, 0)),
            pl.BlockSpec((1, N_KV, tile, LANES), lambda bi, i: (bi, 0, i, 0)),
            pl.BlockSpec((1, N_KV, LANES, tile), lambda bi, i: (bi, 0, 0, i)),
        ],
        out_shape=[
            jax.ShapeDtypeStruct((b, N_PAIRS, s, LANES), BF16),
            jax.ShapeDtypeStruct((b, N_KV, s, LANES), BF16),
            jax.ShapeDtypeStruct((b, N_KV, LANES, s), BF16),
        ],
        compiler_params=_params(2),
        name="proj",
    )(x, gmix, w_in, gq, gk, bd, ca, sa, cb, sb)


def _attn_kernel(x_ref, q_ref, k_ref, vt_ref, km_ref, vtm_ref, lam_ref, gsub_ref, wo_ref, gffn_ref,
                 h_ref, n_ref, sa_ref, sb_ref, acca_ref, accb_ref, la_ref, lb_ref, mix_ref,
                 *, key_chunk):
    tq = q_ref.shape[2]
    s_len = k_ref.shape[2]
    n_chunks = s_len // key_chunk
    lane = lax.broadcasted_iota(jnp.int32, (tq, LANES), 1)
    low = lane < HEAD_DIM
    nt = (((1,), (1,)), ((), ()))

    lam_p = lam_ref[...]
    lam = (jnp.exp(jnp.sum(lam_p[0:1] * lam_p[1:2], axis=-1, keepdims=True))
           - jnp.exp(jnp.sum(lam_p[2:3] * lam_p[3:4], axis=-1, keepdims=True))
           + LAMBDA_INIT)

    def chunk(c):
        return slice(c * key_chunk, (c + 1) * key_chunk)

    def stage(scores_job=None, softmax_job=None, finish_job=None):
        if scores_job is not None:
            b1, j1, buf1 = scores_job
            p1 = j1 + A_PAIRS if b1 else j1
            kv1 = j1 + 1 if b1 else 0
            qp = q_ref[0, p1].astype(F32)
            qq = jnp.concatenate([jnp.where(low, qp, 0.0), jnp.where(low, 0.0, qp)],
                                 axis=0).astype(BF16)
            sm1 = lax.dot_general(km_ref[0, kv1][:N_META], qq, nt, preferred_element_type=F32)
            m8 = jnp.maximum(sm1[:8], sm1[8:])
        if softmax_job is not None:
            b2, j2, buf2, (sm2, m2) = softmax_job
            kv2 = j2 + 1 if b2 else 0
            pm = jnp.exp2(sm2 - m2)
            l8 = pm[:8] + pm[8:]
            pm_pad = jnp.concatenate(
                [pm.astype(BF16), jnp.zeros((META_ROWS - N_META, 2 * tq), BF16)], axis=0)
            acc = jnp.dot(vtm_ref[0, kv2], pm_pad, preferred_element_type=F32)
        out = None
        if finish_job is not None:
            b3, j3 = finish_job
            p3 = j3 + A_PAIRS if b3 else j3
            acc_ref, l_ref = (accb_ref, lb_ref) if b3 else (acca_ref, la_ref)
            inv_l = 1.0 / jnp.sum(l_ref[...], axis=0, keepdims=True)
            ot = (acc_ref[...] * inv_l).T
            o_lo, o_hi = ot[:tq], ot[tq:]
            if b3:
                dt = o_lo - lam * o_hi
                msd = jnp.mean(dt * dt, axis=-1, keepdims=True)
                out = (dt * lax.rsqrt(msd + NORM_EPS) * gsub_ref[...]
                       * (1.0 - LAMBDA_INIT)).astype(BF16)
            else:
                out = jnp.where(low, o_lo, o_hi).astype(BF16)
            mix_ref[p3] = out
        for c in range(n_chunks):
            if scores_job is not None:
                sc = lax.dot_general(k_ref[0, kv1, chunk(c), :], qq, nt,
                                     preferred_element_type=F32)
                buf1[chunk(c), :] = sc
                m8 = jnp.maximum(m8, jnp.max(sc.reshape(key_chunk // 8, 8, 2 * tq), axis=0))
            if softmax_job is not None:
                pc = jnp.exp2(buf2[chunk(c), :] - m2)
                l8 = l8 + jnp.sum(pc.reshape(key_chunk // 8, 8, 2 * tq), axis=0)
                acc = acc + jnp.dot(vt_ref[0, kv2, :, chunk(c)], pc.astype(BF16),
                                    preferred_element_type=F32)
        if softmax_job is not None:
            acc_ref, l_ref = (accb_ref, lb_ref) if b2 else (acca_ref, la_ref)
            acc_ref[...] = acc
            l_ref[...] = l8
        if scores_job is not None:
            return sm1, jnp.max(m8, axis=0, keepdims=True)
        return out

    grp_a, grp_b = False, True
    last = A_PAIRS - 1
    stats = stage(scores_job=(grp_a, 0, sa_ref))
    stats = stage(scores_job=(grp_b, 0, sb_ref), softmax_job=(grp_a, 0, sa_ref, stats))
    stats = stage(scores_job=(grp_a, 1, sa_ref), softmax_job=(grp_b, 0, sb_ref, stats),
                  finish_job=(grp_a, 0))

    def body(j, stats_a):
        stats_b = stage(scores_job=(grp_b, j, sb_ref), softmax_job=(grp_a, j, sa_ref, stats_a),
                        finish_job=(grp_b, j - 1))
        return stage(scores_job=(grp_a, j + 1, sa_ref), softmax_job=(grp_b, j, sb_ref, stats_b),
                     finish_job=(grp_a, j))

    stats = lax.fori_loop(1, last, body, stats)
    stats = stage(scores_job=(grp_b, last, sb_ref), softmax_job=(grp_a, last, sa_ref, stats),
                  finish_job=(grp_b, last - 1))
    stage(softmax_job=(grp_b, last, sb_ref, stats), finish_job=(grp_a, last))
    out_last = stage(finish_job=(grp_b, last))

    split = (N_PAIRS - 1) * LANES
    mix = jnp.concatenate([mix_ref[p] for p in range(N_PAIRS - 1)], axis=1)
    h = (x_ref[0]
         + jnp.dot(mix, wo_ref[:split, :], preferred_element_type=F32)
         + jnp.dot(out_last, wo_ref[split:, :], preferred_element_type=F32))
    h_ref[0] = h
    ms = jnp.mean(h * h, axis=-1, keepdims=True)
    n_ref[0] = (h * lax.rsqrt(ms + NORM_EPS) * gffn_ref[...]).astype(BF16)


def _attn_call(x, q, k, vt, km, vtm, lam_p, gsub, w_out, g_ffn, tq, key_chunk, shared_q):
    _, _, sq, _ = q.shape
    b, _, s, _ = k.shape
    d = x.shape[-1]
    grid = (b, sq // tq)
    if shared_q:
        q_map = lambda bi, i: (0, 0, i, 0)
        x_map = lambda bi, i: (0, i, 0)
    else:
        q_map = lambda bi, i: (bi, 0, i, 0)
        x_map = lambda bi, i: (bi, i, 0)
    return pl.pallas_call(
        functools.partial(_attn_kernel, key_chunk=key_chunk),
        grid=grid,
        in_specs=[
            pl.BlockSpec((1, tq, d), x_map),
            pl.BlockSpec((1, N_PAIRS, tq, LANES), q_map),
            pl.BlockSpec((1, N_KV, s, LANES), lambda bi, i: (bi, 0, 0, 0)),
            pl.BlockSpec((1, N_KV, LANES, s), lambda bi, i: (bi, 0, 0, 0)),
            _const_spec((1, N_KV, META_ROWS, LANES)),
            _const_spec((1, N_KV, LANES, META_ROWS)),
            _const_spec((8, LANES)),
            _const_spec((1, LANES)),
            _const_spec((d, d)),
            _const_spec((1, d)),
        ],
        out_specs=[
            pl.BlockSpec((1, tq, d), lambda bi, i: (bi, i, 0)),
            pl.BlockSpec((1, tq, d), lambda bi, i: (bi, i, 0)),
        ],
        out_shape=[
            jax.ShapeDtypeStruct((b, sq, d), F32),
            jax.ShapeDtypeStruct((b, sq, d), BF16),
        ],
        scratch_shapes=[
            pltpu.VMEM((s, 2 * tq), F32),
            pltpu.VMEM((s, 2 * tq), F32),
            pltpu.VMEM((LANES, 2 * tq), F32),
            pltpu.VMEM((LANES, 2 * tq), F32),
            pltpu.VMEM((8, 2 * tq), F32),
            pltpu.VMEM((8, 2 * tq), F32),
            pltpu.VMEM((N_PAIRS, tq, LANES), BF16),
        ],
        compiler_params=_params(2),
        name="attn",
    )(x, q, k, vt, km, vtm, lam_p, gsub, w_out, g_ffn)


def _ffn_kernel(h_ref, n_ref, left_ref, meta_ref, right_ref, wg_ref, wu_ref, cw_ref, wd_ref,
                gfin_ref, o_ref, gate0_ref, gate1_ref, u_ref):
    i = pl.program_id(1)
    last = pl.num_programs(1) - 1
    t = n_ref.shape[1]
    n = n_ref[0]
    left = jnp.where(i == 0, meta_ref[0], left_ref[0])
    right = jnp.where(i == last, jnp.zeros_like(right_ref[0]), right_ref[0])
    n_ext = jnp.concatenate([left, n, right], axis=0)

    for c in range(N_FF_CHUNKS):
        gate_ref = gate1_ref if c % 2 else gate0_ref
        gate_ref[...] = jnp.dot(n_ext, wg_ref[c], preferred_element_type=F32)
        cw = cw_ref[c]
        g = (gate_ref[HALO - 1:HALO - 1 + t, :] * cw[0:1]
             + gate_ref[HALO:HALO + t, :] * cw[1:2]
             + gate_ref[HALO + 1:HALO + 1 + t, :] * cw[2:3]
             + cw[3:4])
        up = jnp.dot(n, wu_ref[c], preferred_element_type=F32)
        act = 0.5 * g * (1.0 + lax.erf(g * (2.0 ** -0.5)))
        u_ref[:, c * FF_CHUNK:(c + 1) * FF_CHUNK] = (act * up).astype(BF16)
    y = h_ref[0] + jnp.dot(u_ref[...], wd_ref[...], preferred_element_type=F32)
    ms = jnp.mean(y * y, axis=-1, keepdims=True)
    o_ref[0] = y * lax.rsqrt(ms + NORM_EPS) * gfin_ref[...]


def _ffn_call(h, n2, n2_meta, wg, wu, cw, wd, g_final, tile):
    b, s, d = h.shape
    grid = (b, s // tile)
    per = tile // HALO
    n_halo_blocks = s // HALO
    return pl.pallas_call(
        _ffn_kernel,
        grid=grid,
        in_specs=[
            pl.BlockSpec((1, tile, d), lambda bi, i: (bi, i, 0)),
            pl.BlockSpec((1, tile, d), lambda bi, i: (bi, i, 0)),
            pl.BlockSpec((1, HALO, d), lambda bi, i: (bi, jnp.maximum(i * per - 1, 0), 0)),
            pl.BlockSpec((1, HALO, d), lambda bi, i: (bi, 0, 0)),
            pl.BlockSpec((1, HALO, d),
                         lambda bi, i: (bi, jnp.minimum((i + 1) * per, n_halo_blocks - 1), 0)),
            _const_spec((N_FF_CHUNKS, d, FF_CHUNK)),
            _const_spec((N_FF_CHUNKS, d, FF_CHUNK)),
            _const_spec((N_FF_CHUNKS, 8, FF_CHUNK)),
            _const_spec((D_FF, d)),
            _const_spec((1, d)),
        ],
        out_specs=pl.BlockSpec((1, tile, d), lambda bi, i: (bi, i, 0)),
        out_shape=jax.ShapeDtypeStruct((b, s, d), F32),
        scratch_shapes=[
            pltpu.VMEM((tile + 2 * HALO, FF_CHUNK), F32),
            pltpu.VMEM((tile + 2 * HALO, FF_CHUNK), F32),
            pltpu.VMEM((tile, D_FF), BF16),
        ],
        compiler_params=_params(2),
        name="ffn",
    )(h, n2, n2, n2_meta, n2, wg, wu, cw, wd, g_final)


def _pair_layout(ang):
    c = jnp.cos(ang)
    s = jnp.sin(ang)
    return jnp.tile(c, (1, 4)), jnp.tile(jnp.concatenate([-s, s], axis=-1), (1, 2))


def _linear_inv():
    return ROPE_THETA ** (-jnp.arange(0, HEAD_DIM, 2, dtype=F32) / HEAD_DIM)


def _real_tables(s):
    t = jnp.arange(s)
    rowp = (t // GRID_W).astype(F32)
    colp = (t % GRID_W).astype(F32)
    axis_dim = HEAD_DIM // 2
    inv_a = ROPE_THETA ** (-jnp.arange(0, axis_dim, 2, dtype=F32) / axis_dim)
    ang_a = jnp.concatenate([rowp[:, None] * inv_a[None], colp[:, None] * inv_a[None]], axis=-1)
    pos = jnp.arange(N_META + s, dtype=F32)[N_META:]
    ang_b = pos[:, None] * _linear_inv()[None]
    return _pair_layout(ang_a) + _pair_layout(ang_b)


def _meta_tables():
    ang_a = jnp.zeros((META_ROWS, HEAD_DIM // 2), F32)
    pos = jnp.arange(META_ROWS, dtype=F32)
    ang_b = pos[:, None] * _linear_inv()[None]
    return _pair_layout(ang_a) + _pair_layout(ang_b)


def _trunk(x, meta_x, meta_qkv, prm, cfg):
    s = x.shape[1]
    q, k, vt = _proj_call(x, _real_tables(s), prm["w_in"], prm["g_mix"], prm["gq"], prm["gk"],
                          prm["bd"], min(s, cfg["proj_tile"]))
    q_m, k_m, vt_m = meta_qkv
    kc = min(s, cfg["key_chunk"])
    h1, n2 = _attn_call(x, q, k, vt, k_m, vt_m, prm["lam"], prm["g_subln"], prm["w_out"],
                        prm["g_ffn"], min(s, cfg["tq"]), kc, shared_q=False)
    _, n2_m = _attn_call(meta_x[:, :META_Q], q_m[:, :, :META_Q], k, vt, k_m, vt_m, prm["lam"],
                         prm["g_subln"], prm["w_out"], prm["g_ffn"], META_Q, kc, shared_q=True)
    return _ffn_call(h1, n2, n2_m, prm["wg"], prm["wu"], prm["cw"], prm["wd"], prm["g_final"],
                     min(s, cfg["row_tile"]))


_CFG = dict(proj_tile=512, tq=256, key_chunk=512, row_tile=512)


def kernel(x_prompt, x_sample, meta_tokens, g_mix, w_in, g_qnorm_a, g_knorm_a, lambda_q1, lambda_k1,
           lambda_q2, lambda_k2, g_subln, w_out, g_ffn, w_ff_gate, w_ff_up, conv_w, conv_b,
           w_ff_down, g_final):
    assert w_in.shape[0] == 1, "single-layer trunk"
    d = D_MODEL
    head_order = np.array([0, 4, 1, 5, 2, 6, 3, 7])
    a_cols = (head_order[:, None] * HEAD_DIM + np.arange(HEAD_DIM)[None]).reshape(-1)
    in_perm = np.concatenate([a_cols, np.arange(A_PAIRS * LANES, IN_COLS)])
    out_perm = np.concatenate([a_cols, np.arange(A_PAIRS * LANES, d)])
    blk = np.arange(2 * LANES) // HEAD_DIM
    cw = jnp.concatenate([conv_w[0], conv_b[0][None], jnp.zeros((4, D_FF), F32)], axis=0)
    lam = jnp.stack([lambda_q1[0], lambda_k1[0], lambda_q2[0], lambda_k2[0]])
    prm = dict(
        w_in=w_in[0][:, in_perm].astype(BF16),
        g_mix=g_mix[0][None],
        gq=jnp.tile(g_qnorm_a[0], 2)[None],
        gk=jnp.tile(g_knorm_a[0], 2)[None],
        bd=jnp.asarray((blk[:, None] == blk[None]) / HEAD_DIM, BF16),
        lam=jnp.zeros((8, LANES), F32).at[:4, :HEAD_DIM].set(lam),
        g_subln=g_subln[0][None],
        w_out=w_out[0][out_perm].astype(BF16),
        g_ffn=g_ffn[0][None],
        wg=w_ff_gate[0].reshape(d, N_FF_CHUNKS, FF_CHUNK).transpose(1, 0, 2).astype(BF16),
        wu=w_ff_up[0].reshape(d, N_FF_CHUNKS, FF_CHUNK).transpose(1, 0, 2).astype(BF16),
        cw=cw.reshape(8, N_FF_CHUNKS, FF_CHUNK).transpose(1, 0, 2),
        wd=w_ff_down[0].astype(BF16),
        g_final=g_final[None],
    )
    meta_x = jnp.zeros((1, META_ROWS, d), F32).at[0, :N_META].set(meta_tokens)
    meta_qkv = _proj_call(meta_x, _meta_tables(), prm["w_in"], prm["g_mix"], prm["gq"], prm["gk"],
                          prm["bd"], META_ROWS)
    y_prompt = _trunk(x_prompt, meta_x, meta_qkv, prm, _CFG)
    y_sample = _trunk(x_sample, meta_x, meta_qkv, prm, _CFG)
    return (y_prompt, y_sample)
```

```python
import functools
import math

import jax
import jax.numpy as jnp
import numpy as np
from jax import lax
from jax.experimental import pallas as pl
from jax.experimental.pallas import tpu as pltpu

F32 = jnp.float32
BF16 = jnp.bfloat16

D_MODEL = 1024
HEAD_DIM = 64
N_META = 16
GRID_W = 64
ROPE_THETA = 10000.0
NORM_EPS = 1e-6
LANES = 128
N_PAIRS = 8
N_KV = 5
A_PAIRS = 4
D_FF = 2816
FF_CHUNK = 256
N_FF_CHUNKS = D_FF // FF_CHUNK
IN_COLS = 2304
HALO = 16
META_ROWS = 128
META_Q = 64
LAMBDA_INIT = 0.8 - 0.6 * math.exp(-0.3 * 0)
Q_SCALE = (HEAD_DIM ** -0.5) * math.log2(math.e)
VMEM_LIMIT = 56 * 1024 * 1024


def _params(n_grid_axes):
    return pltpu.CompilerParams(
        dimension_semantics=("arbitrary",) * n_grid_axes,
        vmem_limit_bytes=VMEM_LIMIT,
    )


def _const_spec(shape):
    zeros = (0,) * len(shape)
    return pl.BlockSpec(shape, lambda *_: zeros, pipeline_mode=pl.Buffered(1))


def _proj_kernel(x_ref, gmix_ref, w_ref, gq_ref, gk_ref, bd_ref, ca_ref, sa_ref, cb_ref, sb_ref,
                 q_ref, k_ref, vt_ref):
    x = x_ref[0]
    t = x.shape[0]
    ms = jnp.mean(x * x, axis=-1, keepdims=True)
    n = (x * lax.rsqrt(ms + NORM_EPS) * gmix_ref[...]).astype(BF16)

    def rope(y, c, s):
        return y * c + pltpu.roll(y, LANES // 2, 1) * s

    def proj(col):
        y = jnp.dot(n, w_ref[:, col:col + 2 * LANES], preferred_element_type=F32)
        return y[:, :LANES], y[:, LANES:]

    ca, sa, cb, sb = ca_ref[...], sa_ref[...], cb_ref[...], sb_ref[...]
    gq, gk = gq_ref[...], gk_ref[...]

    ya = list(proj(0) + proj(2 * LANES))
    yk, yv = proj(4 * LANES)
    vt_ref[0, 0] = yv.T.astype(BF16)
    sq = jnp.concatenate([(y * y).astype(BF16) for y in ya + [yk]], axis=0)
    ms_all = jnp.dot(sq, bd_ref[...], preferred_element_type=F32)
    for p in range(A_PAIRS):
        r = lax.rsqrt(ms_all[p * t:(p + 1) * t] + NORM_EPS) * Q_SCALE
        q_ref[0, p] = (rope(ya[p] * gq, ca, sa) * r).astype(BF16)
    r = lax.rsqrt(ms_all[A_PAIRS * t:] + NORM_EPS)
    k_ref[0, 0] = (rope(yk * gk, ca, sa) * r).astype(BF16)
    for j in range(2):
        y0, y1 = proj(14 * LANES + 2 * LANES * j)
        vt_ref[0, 1 + 2 * j] = y0.T.astype(BF16)
        vt_ref[0, 2 + 2 * j] = y1.T.astype(BF16)
    for j in range(2):
        y0, y1 = proj(10 * LANES + 2 * LANES * j)
        k_ref[0, 1 + 2 * j] = rope(y0, cb, sb).astype(BF16)
        k_ref[0, 2 + 2 * j] = rope(y1, cb, sb).astype(BF16)
    for j in range(2):
        y0, y1 = proj(6 * LANES + 2 * LANES * j)
        q_ref[0, A_PAIRS + 2 * j] = (rope(y0, cb, sb) * Q_SCALE).astype(BF16)
        q_ref[0, A_PAIRS + 2 * j + 1] = (rope(y1, cb, sb) * Q_SCALE).astype(BF16)


def _proj_call(x, tables, w_in, gmix, gq, gk, bd, tile):
    b, s, d = x.shape
    ca, sa, cb, sb = tables
    grid = (b, s // tile)
    tab_spec = pl.BlockSpec((tile, LANES), lambda bi, i: (i, 0))
    return pl.pallas_call(
        _proj_kernel,
        grid=grid,
        in_specs=[
            pl.BlockSpec((1, tile, d), lambda bi, i: (bi, i, 0)),
            _const_spec((1, d)),
            _const_spec((d, IN_COLS)),
            _const_spec((1, LANES)),
            _const_spec((1, LANES)),
            _const_spec((LANES, LANES)),
            tab_spec, tab_spec, tab_spec, tab_spec,
        ],
        out_specs=[
            pl.BlockSpec((1, N_PAIRS, tile, LANES), lambda bi, i: (bi, 0, i, 0)),
            pl.BlockSpec((1, N_KV, tile, LANES), lambda bi, i: (bi, 0, i, 0)),
            pl.BlockSpec((1, N_KV, LANES, tile), lambda bi, i: (bi, 0, 0, i)),
        ],
        out_shape=[
            jax.ShapeDtypeStruct((b, N_PAIRS, s, LANES), BF16),
            jax.ShapeDtypeStruct((b, N_KV, s, LANES), BF16),
            jax.ShapeDtypeStruct((b, N_KV, LANES, s), BF16),
        ],
        compiler_params=_params(2),
        name="proj",
    )(x, gmix, w_in, gq, gk, bd, ca, sa, cb, sb)


def _attn_kernel(x_ref, q_ref, k_ref, vt_ref, km_ref, vtm_ref, lam_ref, gsub_ref, wo_ref, gffn_ref,
                 h_ref, n_ref, sa_ref, sb_ref, acca_ref, accb_ref, la_ref, lb_ref, mix_ref,
                 *, key_chunk):
    tq = q_ref.shape[2]
    s_len = k_ref.shape[2]
    n_chunks = s_len // key_chunk
    lane = lax.broadcasted_iota(jnp.int32, (tq, LANES), 1)
    q_low = (lane & (HEAD_DIM // 2)) == 0
    low = lane < HEAD_DIM
    nt =(((1,), (1,)), ((), ()))

    lam_p = lam_ref[...]
    lam = (jnp.exp(jnp.sum(lam_p[0:1] * lam_p[1:2], axis=-1, keepdims=True))
           - jnp.exp(jnp.sum(lam_p[2:3] * lam_p[3:4], axis=-1, keepdims=True))
           + LAMBDA_INIT)

    def chunk(c):
        return slice(c * key_chunk, (c + 1) * key_chunk)

    def stage(scores_job=None, softmax_job=None, finish_job=None):
        if scores_job is not None:
            b1, j1, buf1 = scores_job
            p1 = j1 + A_PAIRS if b1 else j1
            kv1 = j1 + 1 if b1 else 0
            qp = q_ref[0, p1].astype(F32)
            qq = jnp.concatenate([jnp.where(q_low, qp, 0.0), jnp.where(q_low, 0.0, qp)],
                                 axis=0).astype(BF16)
            sm1 = lax.dot_general(km_ref[0, kv1][:N_META], qq, nt, preferred_element_type=F32)
            m8 = jnp.maximum(sm1[:8], sm1[8:])
        if softmax_job is not None:
            b2, j2, buf2, (sm2, m2) = softmax_job
            kv2 = j2 + 1 if b2 else 0
            pm = jnp.exp2(sm2 - m2)
            l8 = pm[:8] + pm[8:]
            pm_pad = jnp.concatenate(
                [pm.astype(BF16), jnp.zeros((META_ROWS - N_META, 2 * tq), BF16)], axis=0)
            acc = jnp.dot(vtm_ref[0, kv2], pm_pad, preferred_element_type=F32)
        out = None
        if finish_job is not None:
            b3, j3 = finish_job
            p3 = j3 + A_PAIRS if b3 else j3
            acc_ref, l_ref = (accb_ref, lb_ref) if b3 else (acca_ref, la_ref)
            inv_l = 1.0 / jnp.sum(l_ref[...], axis=0, keepdims=True)
            ot = (acc_ref[...] * inv_l).T
            o_lo, o_hi = ot[:tq], ot[tq:]
            if b3:
                dt = o_lo - lam * o_hi
                msd = jnp.mean(dt * dt, axis=-1, keepdims=True)
                out = (dt * lax.rsqrt(msd + NORM_EPS) * gsub_ref[...]
                       * (1.0 - LAMBDA_INIT)).astype(BF16)
            else:
                out = jnp.where(low, o_lo, o_hi).astype(BF16)
            mix_ref[p3] = out
        for c in range(n_chunks):
            if scores_job is not None:
                sc = lax.dot_general(k_ref[0, kv1, chunk(c), :], qq, nt,
                                     preferred_element_type=F32)
                buf1[chunk(c), :] = sc
                m8 = jnp.maximum(m8, jnp.max(sc.reshape(key_chunk // 8, 8, 2 * tq), axis=0))
            if softmax_job is not None:
                pc = jnp.exp2(buf2[chunk(c), :] - m2)
                l8 = l8 + jnp.sum(pc.reshape(key_chunk // 8, 8, 2 * tq), axis=0)
                acc = acc + jnp.dot(vt_ref[0, kv2, :, chunk(c)], pc.astype(BF16),
                                    preferred_element_type=F32)
        if softmax_job is not None:
            acc_ref, l_ref = (accb_ref, lb_ref) if b2 else (acca_ref, la_ref)
            acc_ref[...] = acc
            l_ref[...] = l8
        if scores_job is not None:
            return sm1, jnp.max(m8, axis=0, keepdims=True)
        return out

    grp_a, grp_b = False, True
    last = A_PAIRS - 1
    stats = stage(scores_job=(grp_a, 0, sa_ref))
    stats = stage(scores_job=(grp_b, 0, sb_ref), softmax_job=(grp_a, 0, sa_ref, stats))
    stats = stage(scores_job=(grp_a, 1, sa_ref), softmax_job=(grp_b, 0, sb_ref, stats),
                  finish_job=(grp_a, 0))

    def body(j, stats_a):
        stats_b = stage(scores_job=(grp_b, j, sb_ref), softmax_job=(grp_a, j, sa_ref, stats_a),
                        finish_job=(grp_b, j - 1))
        return stage(scores_job=(grp_a, j + 1, sa_ref), softmax_job=(grp_b, j, sb_ref, stats_b),
                     finish_job=(grp_a, j))

    stats = lax.fori_loop(1, last, body, stats)
    stats = stage(scores_job=(grp_b, last, sb_ref), softmax_job=(grp_a, last, sa_ref, stats),
                  finish_job=(grp_b, last - 1))
    stage(softmax_job=(grp_b, last, sb_ref, stats), finish_job=(grp_a, last))
    out_last = stage(finish_job=(grp_b, last))

    split = (N_PAIRS - 1) * LANES
    mix = jnp.concatenate([mix_ref[p] for p in range(N_PAIRS - 1)], axis=1)
    h = (x_ref[0]
         + jnp.dot(mix, wo_ref[:split, :], preferred_element_type=F32)
         + jnp.dot(out_last, wo_ref[split:, :], preferred_element_type=F32))
    h_ref[0] = h
    ms = jnp.mean(h * h, axis=-1, keepdims=True)
    n_ref[0] = (h * lax.rsqrt(ms + NORM_EPS) * gffn_ref[...]).astype(BF16)


def _attn_call(x, q, k, vt, km, vtm, lam_p, gsub, w_out, g_ffn, tq, key_chunk, shared_q):
    _, _, sq, _ = q.shape
    b, _, s, _ = k.shape
    d = x.shape[-1]
    grid = (b, sq // tq)
    if shared_q:
        q_map = lambda bi, i: (0, 0, i, 0)
        x_map = lambda bi, i: (0, i, 0)
    else:
        q_map = lambda bi, i: (bi, 0, i, 0)
        x_map = lambda bi, i: (bi, i, 0)
    return pl.pallas_call(
        functools.partial(_attn_kernel, key_chunk=key_chunk),
        grid=grid,
        in_specs=[
            pl.BlockSpec((1, tq, d), x_map),
            pl.BlockSpec((1, N_PAIRS, tq, LANES), q_map),
            pl.BlockSpec((1, N_KV, s, LANES), lambda bi, i: (bi, 0, 0, 0)),
            pl.BlockSpec((1, N_KV, LANES, s), lambda bi, i: (bi, 0, 0, 0)),
            _const_spec((1, N_KV, META_ROWS, LANES)),
            _const_spec((1, N_KV, LANES, META_ROWS)),
            _const_spec((8, LANES)),
            _const_spec((1, LANES)),
            _const_spec((d, d)),
            _const_spec((1, d)),
        ],
        out_specs=[
            pl.BlockSpec((1, tq, d), lambda bi, i: (bi, i, 0)),
            pl.BlockSpec((1, tq, d), lambda bi, i: (bi, i, 0)),
        ],
        out_shape=[
            jax.ShapeDtypeStruct((b, sq, d), F32),
            jax.ShapeDtypeStruct((b, sq, d), BF16),
        ],
        scratch_shapes=[
            pltpu.VMEM((s, 2 * tq), F32),
            pltpu.VMEM((s, 2 * tq), F32),
            pltpu.VMEM((LANES, 2 * tq), F32),
            pltpu.VMEM((LANES, 2 * tq), F32),
            pltpu.VMEM((8, 2 * tq), F32),
            pltpu.VMEM((8, 2 * tq), F32),
            pltpu.VMEM((N_PAIRS, tq, LANES), BF16),
        ],
        compiler_params=_params(2),
        name="attn",
    )(x, q, k, vt, km, vtm, lam_p, gsub, w_out, g_ffn)


def _ffn_kernel(h_ref, n_ref, left_ref, meta_ref, right_ref, wg_ref, wu_ref, cw_ref, wd_ref,
                gfin_ref, o_ref, gate0_ref, gate1_ref, u_ref):
    i = pl.program_id(1)
    last = pl.num_programs(1) - 1
    t = n_ref.shape[1]
    n = n_ref[0]
    left = jnp.where(i == 0, meta_ref[0], left_ref[0])
    right = jnp.where(i == last, jnp.zeros_like(right_ref[0]), right_ref[0])
    n_ext = jnp.concatenate([left, n, right], axis=0)

    for c in range(N_FF_CHUNKS):
        gate_ref = gate1_ref if c % 2 else gate0_ref
        gate_ref[...] = jnp.dot(n_ext, wg_ref[c], preferred_element_type=F32)
        cw = cw_ref[c]
        g = (gate_ref[HALO - 1:HALO - 1 + t, :] * cw[0:1]
             + gate_ref[HALO:HALO + t, :] * cw[1:2]
             + gate_ref[HALO + 1:HALO + 1 + t, :] * cw[2:3]
             + cw[3:4])
        up = jnp.dot(n, wu_ref[c], preferred_element_type=F32)
        act = 0.5 * g * (1.0 + lax.erf(g * (2.0 ** -0.5)))
        u_ref[:, c * FF_CHUNK:(c + 1) * FF_CHUNK] = (act * up).astype(BF16)
    y = h_ref[0] + jnp.dot(u_ref[...], wd_ref[...], preferred_element_type=F32)
    ms = jnp.mean(y * y, axis=-1, keepdims=True)
    o_ref[0] = y * lax.rsqrt(ms + NORM_EPS) * gfin_ref[...]


def _ffn_call(h, n2, n2_meta, wg, wu, cw, wd, g_final, tile):
    b, s, d = h.shape
    grid = (b, s // tile)
    per = tile // HALO
    n_halo_blocks = s // HALO
    return pl.pallas_call(
        _ffn_kernel,
        grid=grid,
        in_specs=[
            pl.BlockSpec((1, tile, d), lambda bi, i: (bi, i, 0)),
            pl.BlockSpec((1, tile, d), lambda bi, i: (bi, i, 0)),
            pl.BlockSpec((1, HALO, d), lambda bi, i: (bi, jnp.maximum(i * per - 1, 0), 0)),
            pl.BlockSpec((1, HALO, d), lambda bi, i: (bi, 0, 0)),
            pl.BlockSpec((1, HALO, d),
                         lambda bi, i: (bi, jnp.minimum((i + 1) * per, n_halo_blocks - 1), 0)),
            _const_spec((N_FF_CHUNKS, d, FF_CHUNK)),
            _const_spec((N_FF_CHUNKS, d, FF_CHUNK)),
            _const_spec((N_FF_CHUNKS, 8, FF_CHUNK)),
            _const_spec((D_FF, d)),
            _const_spec((1, d)),
        ],
        out_specs=pl.BlockSpec((1, tile, d), lambda bi, i: (bi, i, 0)),
        out_shape=jax.ShapeDtypeStruct((b, s, d), F32),
        scratch_shapes=[
            pltpu.VMEM((tile + 2 * HALO, FF_CHUNK), F32),
            pltpu.VMEM((tile + 2 * HALO, FF_CHUNK), F32),
            pltpu.VMEM((tile, D_FF), BF16),
        ],
        compiler_params=_params(2),
        name="ffn",
    )(h, n2, n2, n2_meta, n2, wg, wu, cw, wd, g_final)


def _pair_layout(ang):
    c = jnp.cos(ang)
    s = jnp.sin(ang)
    return jnp.tile(c, (1, 4)), jnp.concatenate([-s, -s, s, s], axis=-1)


def _linear_inv():
    return ROPE_THETA ** (-jnp.arange(0, HEAD_DIM, 2, dtype=F32) / HEAD_DIM)


def _real_tables(s):
    t = jnp.arange(s)
    rowp = (t // GRID_W).astype(F32)
    colp = (t % GRID_W).astype(F32)
    axis_dim = HEAD_DIM // 2
    inv_a = ROPE_THETA ** (-jnp.arange(0, axis_dim, 2, dtype=F32) / axis_dim)
    ang_a = jnp.concatenate([rowp[:, None] * inv_a[None], colp[:, None] * inv_a[None]], axis=-1)
    pos = jnp.arange(N_META + s, dtype=F32)[N_META:]
    ang_b = pos[:, None] * _linear_inv()[None]
    return _pair_layout(ang_a) + _pair_layout(ang_b)


def _meta_tables():
    ang_a = jnp.zeros((META_ROWS, HEAD_DIM // 2), F32)
    pos = jnp.arange(META_ROWS, dtype=F32)
    ang_b = pos[:, None] * _linear_inv()[None]
    return _pair_layout(ang_a) + _pair_layout(ang_b)


def _trunk(x, meta_x, meta_qkv, prm, cfg):
    s = x.shape[1]
    q, k, vt = _proj_call(x, _real_tables(s), prm["w_in"], prm["g_mix"], prm["gq"], prm["gk"],
                          prm["bd"], min(s, cfg["proj_tile"]))
    q_m, k_m, vt_m = meta_qkv
    kc = min(s, cfg["key_chunk"])
    h1, n2 = _attn_call(x, q, k, vt, k_m, vt_m, prm["lam"], prm["g_subln"], prm["w_out"],
                        prm["g_ffn"], min(s, cfg["tq"]), kc, shared_q=False)
    _, n2_m = _attn_call(meta_x[:, :META_Q], q_m[:, :, :META_Q], k, vt, k_m, vt_m, prm["lam"],
                         prm["g_subln"], prm["w_out"], prm["g_ffn"], META_Q, kc, shared_q=True)
    return _ffn_call(h1, n2, n2_m, prm["wg"], prm["wu"], prm["cw"], prm["wd"], prm["g_final"],
                     min(s, cfg["row_tile"]))


_CFG = dict(proj_tile=512, tq=256, key_chunk=512, row_tile=512)


def kernel(x_prompt, x_sample, meta_tokens, g_mix, w_in, g_qnorm_a, g_knorm_a, lambda_q1, lambda_k1,
           lambda_q2, lambda_k2, g_subln, w_out, g_ffn, w_ff_gate, w_ff_up, conv_w, conv_b,
           w_ff_down, g_final):
    assert w_in.shape[0] == 1, "single-layer trunk"
    d = D_MODEL
    head_order = np.array([0, 4, 1, 5, 2, 6, 3, 7])
    a_cols = (head_order[:, None] * HEAD_DIM + np.arange(HEAD_DIM)[None]).reshape(-1)
    in_perm = np.concatenate([a_cols, np.arange(A_PAIRS * LANES, IN_COLS)])
    out_perm = np.concatenate([a_cols, np.arange(A_PAIRS * LANES, d)])
    half = HEAD_DIM // 2
    tile_perm = np.concatenate([np.arange(0, half), np.arange(2 * half, 3 * half),
                                np.arange(half, 2 * half), np.arange(3 * half, 4 * half)])
    qk_tiles = list(range(0, 5)) + list(range(6, 14))
    for tile in qk_tiles:
        in_perm[tile * LANES:(tile + 1) * LANES] = in_perm[tile * LANES + tile_perm]
    lane_dim = np.concatenate([np.arange(half), np.arange(half),
                               np.arange(half, 2 * half), np.arange(half, 2 * half)])
    blk = (np.arange(LANES) // half) % 2
    cw = jnp.concatenate([conv_w[0], conv_b[0][None], jnp.zeros((4, D_FF), F32)], axis=0)
    lam = jnp.stack([lambda_q1[0], lambda_k1[0], lambda_q2[0], lambda_k2[0]])
    prm = dict(
        w_in=w_in[0][:, in_perm].astype(BF16),
        g_mix=g_mix[0][None],
        gq=g_qnorm_a[0][lane_dim][None],
        gk=g_knorm_a[0][lane_dim][None],
        bd=jnp.asarray((blk[:, None] == blk[None]) / HEAD_DIM, BF16),
        lam=jnp.zeros((8, LANES), F32).at[:4, :HEAD_DIM].set(lam),
        g_subln=g_subln[0][None],
        w_out=w_out[0][out_perm].astype(BF16),
        g_ffn=g_ffn[0][None],
        wg=w_ff_gate[0].reshape(d, N_FF_CHUNKS, FF_CHUNK).transpose(1, 0, 2).astype(BF16),
        wu=w_ff_up[0].reshape(d, N_FF_CHUNKS, FF_CHUNK).transpose(1, 0, 2).astype(BF16),
        cw=cw.reshape(8, N_FF_CHUNKS, FF_CHUNK).transpose(1, 0, 2),
        wd=w_ff_down[0].astype(BF16),
        g_final=g_final[None],
    )
    meta_x = jnp.zeros((1, META_ROWS, d), F32).at[0, :N_META].set(meta_tokens)
    meta_qkv = _proj_call(meta_x, _meta_tables(), prm["w_in"], prm["g_mix"], prm["gq"], prm["gk"],
                          prm["bd"], META_ROWS)
    y_prompt = _trunk(x_prompt, meta_x, meta_qkv, prm, _CFG)
    y_sample = _trunk(x_sample, meta_x, meta_qkv, prm, _CFG)
    return (y_prompt, y_sample)
```

```python
import functools
import math

import jax
import jax.numpy as jnp
import numpy as np
from jax import lax
from jax.experimental import pallas as pl
from jax.experimental.pallas import tpu as pltpu

F32 = jnp.float32
BF16 = jnp.bfloat16

D_MODEL = 1024
HEAD_DIM = 64
N_META = 16
GRID_W = 64
ROPE_THETA = 10000.0
NORM_EPS = 1e-6
LANES = 128
N_PAIRS = 8
N_KV = 5
A_PAIRS = 4
D_FF = 2816
FF_CHUNK = 256
N_FF_CHUNKS = D_FF // FF_CHUNK
IN_COLS = 2304
HALO = 16
META_ROWS = 128
META_Q = 64
LAMBDA_INIT = 0.8 - 0.6 * math.exp(-0.3 * 0)
Q_SCALE = (HEAD_DIM ** -0.5) * math.log2(math.e)
VMEM_LIMIT = 56 * 1024 * 1024
ATTN_TEMP_BYTES = 11 * 1024 * 1024


def _params(n_grid_axes):
    return pltpu.CompilerParams(
        dimension_semantics=("arbitrary",) * n_grid_axes,
        vmem_limit_bytes=VMEM_LIMIT,
    )


def _const_spec(shape):
    zeros = (0,) * len(shape)
    return pl.BlockSpec(shape, lambda *_: zeros, pipeline_mode=pl.Buffered(1))


def _proj_kernel(x_ref, gmix_ref, w_ref, gq_ref, gk_ref, bd_ref, ca_ref, sa_ref, cb_ref, sb_ref,
                 q_ref, k_ref, vt_ref):
    x = x_ref[0]
    t = x.shape[0]
    ms = jnp.mean(x * x, axis=-1, keepdims=True)
    n = (x * lax.rsqrt(ms + NORM_EPS) * gmix_ref[...]).astype(BF16)

    def rope(y, c, s):
        return y * c + pltpu.roll(y, LANES // 2, 1) * s

    def proj(col):
        y = jnp.dot(n, w_ref[:, col:col + 2 * LANES], preferred_element_type=F32)
        return y[:, :LANES], y[:, LANES:]

    ca, sa, cb, sb = ca_ref[...], sa_ref[...], cb_ref[...], sb_ref[...]
    gq, gk = gq_ref[...], gk_ref[...]

    ya = list(proj(0) + proj(2 * LANES))
    yk, yv = proj(4 * LANES)
    vt_ref[0, 0] = yv.T.astype(BF16)
    sq = jnp.concatenate([(y * y).astype(BF16) for y in ya + [yk]], axis=0)
    ms_all = jnp.dot(sq, bd_ref[...], preferred_element_type=F32)
    for p in range(A_PAIRS):
        r = lax.rsqrt(ms_all[p * t:(p + 1) * t] + NORM_EPS) * Q_SCALE
        q_ref[0, p] = (rope(ya[p] * gq, ca, sa) * r).astype(BF16)
    r = lax.rsqrt(ms_all[A_PAIRS * t:] + NORM_EPS)
    k_ref[0, 0] = (rope(yk * gk, ca, sa) * r).astype(BF16)
    for j in range(2):
        y0, y1 = proj(14 * LANES + 2 * LANES * j)
        vt_ref[0, 1 + 2 * j] = y0.T.astype(BF16)
        vt_ref[0, 2 + 2 * j] = y1.T.astype(BF16)
    for j in range(2):
        y0, y1 = proj(10 * LANES + 2 * LANES * j)
        k_ref[0, 1 + 2 * j] = rope(y0, cb, sb).astype(BF16)
        k_ref[0, 2 + 2 * j] = rope(y1, cb, sb).astype(BF16)
    for j in range(2):
        y0, y1 = proj(6 * LANES + 2 * LANES * j)
        q_ref[0, A_PAIRS + 2 * j] = (rope(y0, cb, sb) * Q_SCALE).astype(BF16)
        q_ref[0, A_PAIRS + 2 * j + 1] = (rope(y1, cb, sb) * Q_SCALE).astype(BF16)


def _proj_call(x, tables, w_in, gmix, gq, gk, bd, tile):
    b, s, d = x.shape
    ca, sa, cb, sb = tables
    grid = (b, s // tile)
    tab_spec = pl.BlockSpec((tile, LANES), lambda bi, i: (i, 0))
    return pl.pallas_call(
        _proj_kernel,
        grid=grid,
        in_specs=[
            pl.BlockSpec((1, tile, d), lambda bi, i: (bi, i, 0)),
            _const_spec((1, d)),
            _const_spec((d, IN_COLS)),
            _const_spec((1, LANES)),
            _const_spec((1, LANES)),
            _const_spec((LANES, LANES)),
            tab_spec, tab_spec, tab_spec, tab_spec,
        ],
        out_specs=[
            pl.BlockSpec((1, N_PAIRS, tile, LANES), lambda bi, i: (bi, 0, i, 0)),
            pl.BlockSpec((1, N_KV, tile, LANES), lambda bi, i: (bi, 0, i, 0)),
            pl.BlockSpec((1, N_KV, LANES, tile), lambda bi, i: (bi, 0, 0, i)),
        ],
        out_shape=[
            jax.ShapeDtypeStruct((b, N_PAIRS, s, LANES), BF16),
            jax.ShapeDtypeStruct((b, N_KV, s, LANES), BF16),
            jax.ShapeDtypeStruct((b, N_KV, LANES, s), BF16),
        ],
        compiler_params=_params(2),
        name="proj",
    )(x, gmix, w_in, gq, gk, bd, ca, sa, cb, sb)


def _attn_kernel(x_ref, q_ref, k_ref, vt_ref, km_ref, vtm_ref, lam_ref, gsub_ref, wo_ref, gffn_ref,
                 h_ref, n_ref, sa_ref, sb_ref, acca_ref, accb_ref, la_ref, lb_ref, mix_ref,
                 *, tq, key_chunk):
    n_tiles = q_ref.shape[2] // tq
    s_len = k_ref.shape[2]
    n_chunks = s_len // key_chunk
    lane = lax.broadcasted_iota(jnp.int32, (tq, LANES), 1)
    q_low = (lane & (HEAD_DIM // 2)) == 0
    low = lane < HEAD_DIM
    contract_lanes = (((1,), (1,)), ((), ()))

    lam_p = lam_ref[...]
    lam = (jnp.exp(jnp.sum(lam_p[0:1] * lam_p[1:2], axis=-1, keepdims=True))
           - jnp.exp(jnp.sum(lam_p[2:3] * lam_p[3:4], axis=-1, keepdims=True))
           + LAMBDA_INIT)

    def chunk(c):
        return slice(c * key_chunk, (c + 1) * key_chunk)

    def locate(is_b, i):
        if isinstance(i, int):
            t, j = divmod(i, A_PAIRS)
        else:
            t, j = lax.shift_right_logical(i, 2), lax.bitwise_and(i, A_PAIRS - 1)
        return t, (j + A_PAIRS if is_b else j), (j + 1 if is_b else 0)

    def stage(scores_job=None, softmax_job=None, finish_job=None):
        if scores_job is not None:
            b1, i1, buf1 = scores_job
            t1, p1, kv1 = locate(b1, i1)
            row0 = t1 * tq if isinstance(t1, int) else pl.multiple_of(t1 * tq, tq)
            qp = q_ref[0, p1, pl.ds(row0, tq), :].astype(F32)
            qq = jnp.concatenate([jnp.where(q_low, qp, 0.0), jnp.where(q_low, 0.0, qp)],
                                 axis=0).astype(BF16)
            sm1 = lax.dot_general(km_ref[0, kv1][:N_META], qq, contract_lanes,
                                  preferred_element_type=F32)
            m8 = jnp.maximum(sm1[:8], sm1[8:])
        if softmax_job is not None:
            b2, i2, buf2, (sm2, m2) = softmax_job
            _, _, kv2 = locate(b2, i2)
            pm = jnp.exp2(sm2 - m2)
            l8 = pm[:8] + pm[8:]
            pm_pad = jnp.concatenate(
                [pm.astype(BF16), jnp.zeros((META_ROWS - N_META, 2 * tq), BF16)], axis=0)
            acc = jnp.dot(vtm_ref[0, kv2], pm_pad, preferred_element_type=F32)
        out = None
        if finish_job is not None:
            b3, i3 = finish_job
            t3, p3, _ = locate(b3, i3)
            acc_ref, l_ref = (accb_ref, lb_ref) if b3 else (acca_ref, la_ref)
            inv_l = 1.0 / jnp.sum(l_ref[...], axis=0, keepdims=True)
            ot = (acc_ref[...] * inv_l).T
            o_lo, o_hi = ot[:tq], ot[tq:]
            if b3:
                dt = o_lo - lam * o_hi
                msd = jnp.mean(dt * dt, axis=-1, keepdims=True)
                out = (dt * lax.rsqrt(msd + NORM_EPS) * gsub_ref[...]
                       * (1.0 - LAMBDA_INIT)).astype(BF16)
            else:
                out = jnp.where(low, o_lo, o_hi).astype(BF16)
            mix_ref[t3, p3] = out
        for c in range(n_chunks):
            if scores_job is not None:
                sc = lax.dot_general(k_ref[0, kv1, chunk(c), :], qq, contract_lanes,
                                     preferred_element_type=F32)
                buf1[chunk(c), :] = sc
                m8 = jnp.maximum(m8, jnp.max(sc.reshape(key_chunk // 8, 8, 2 * tq), axis=0))
            if softmax_job is not None:
                pc = jnp.exp2(buf2[chunk(c), :] - m2)
                l8 = l8 + jnp.sum(pc.reshape(key_chunk // 8, 8, 2 * tq), axis=0)
                acc = acc + jnp.dot(vt_ref[0, kv2, :, chunk(c)], pc.astype(BF16),
                                    preferred_element_type=F32)
        if softmax_job is not None:
            acc_ref, l_ref = (accb_ref, lb_ref) if b2 else (acca_ref, la_ref)
            acc_ref[...] = acc
            l_ref[...] = l8
        if scores_job is not None:
            return sm1, jnp.max(m8, axis=0, keepdims=True)
        return out

    grp_a, grp_b = False, True
    last = n_tiles * A_PAIRS - 1
    stats = stage(scores_job=(grp_a, 0, sa_ref))
    stats = stage(scores_job=(grp_b, 0, sb_ref), softmax_job=(grp_a, 0, sa_ref, stats))
    stats = stage(scores_job=(grp_a, 1, sa_ref), softmax_job=(grp_b, 0, sb_ref, stats),
                  finish_job=(grp_a, 0))

    def body(i, stats_a):
        stats_b = stage(scores_job=(grp_b, i, sb_ref), softmax_job=(grp_a, i, sa_ref, stats_a),
                        finish_job=(grp_b, i - 1))
        return stage(scores_job=(grp_a, i + 1, sa_ref), softmax_job=(grp_b, i, sb_ref, stats_b),
                     finish_job=(grp_a, i))

    stats = lax.fori_loop(1, last, body, stats)
    stats = stage(scores_job=(grp_b, last, sb_ref), softmax_job=(grp_a, last, sa_ref, stats),
                  finish_job=(grp_b, last - 1))
    stage(softmax_job=(grp_b, last, sb_ref, stats), finish_job=(grp_a, last))
    out_last = stage(finish_job=(grp_b, last))

    split = (N_PAIRS - 1) * LANES
    for t in range(n_tiles):
        rows = slice(t * tq, (t + 1) * tq)
        if t + 1 < n_tiles:
            mix = jnp.concatenate([mix_ref[t, p] for p in range(N_PAIRS)], axis=1)
            h = x_ref[0, rows, :] + jnp.dot(mix, wo_ref[...], preferred_element_type=F32)
        else:
            mix = jnp.concatenate([mix_ref[t, p] for p in range(N_PAIRS - 1)], axis=1)
            h = (x_ref[0, rows, :]
                 + jnp.dot(mix, wo_ref[:split, :], preferred_element_type=F32)
                 + jnp.dot(out_last, wo_ref[split:, :], preferred_element_type=F32))
        h_ref[0, rows, :] = h
        ms = jnp.mean(h * h, axis=-1, keepdims=True)
        n_ref[0, rows, :] = (h * lax.rsqrt(ms + NORM_EPS) * gffn_ref[...]).astype(BF16)


def _attn_call(x, q, k, vt, km, vtm, lam_p, gsub, w_out, g_ffn, tq, n_tiles, key_chunk, shared_q):
    _, _, sq, _ = q.shape
    b, _, s, _ = k.shape
    d = x.shape[-1]
    rows = tq * n_tiles
    grid = (b, sq // rows)
    if shared_q:
        q_map = lambda bi, i: (0, 0, i, 0)
        x_map = lambda bi, i: (0, i, 0)
    else:
        q_map = lambda bi, i: (bi, 0, i, 0)
        x_map = lambda bi, i: (bi, i, 0)
    return pl.pallas_call(
        functools.partial(_attn_kernel, tq=tq, key_chunk=key_chunk),
        grid=grid,
        in_specs=[
            pl.BlockSpec((1, rows, d), x_map),
            pl.BlockSpec((1, N_PAIRS, rows, LANES), q_map),
            pl.BlockSpec((1, N_KV, s, LANES), lambda bi, i: (bi, 0, 0, 0)),
            pl.BlockSpec((1, N_KV, LANES, s), lambda bi, i: (bi, 0, 0, 0)),
            _const_spec((1, N_KV, META_ROWS, LANES)),
            _const_spec((1, N_KV, LANES, META_ROWS)),
            _const_spec((8, LANES)),
            _const_spec((1, LANES)),
            _const_spec((d, d)),
            _const_spec((1, d)),
        ],
        out_specs=[
            pl.BlockSpec((1, rows, d), lambda bi, i: (bi, i, 0)),
            pl.BlockSpec((1, rows, d), lambda bi, i: (bi, i, 0)),
        ],
        out_shape=[
            jax.ShapeDtypeStruct((b, sq, d), F32),
            jax.ShapeDtypeStruct((b, sq, d), BF16),
        ],
        scratch_shapes=[
            pltpu.VMEM((s, 2 * tq), F32),
            pltpu.VMEM((s, 2 * tq), F32),
            pltpu.VMEM((LANES, 2 * tq), F32),
            pltpu.VMEM((LANES, 2 * tq), F32),
            pltpu.VMEM((8, 2 * tq), F32),
            pltpu.VMEM((8, 2 * tq), F32),
            pltpu.VMEM((n_tiles, N_PAIRS, tq, LANES), BF16),
        ],
        compiler_params=_params(2),
        name="attn",
    )(x, q, k, vt, km, vtm, lam_p, gsub, w_out, g_ffn)


def _ffn_kernel(h_ref, n_ref, left_ref, meta_ref, right_ref, wg_ref, wu_ref, cw_ref, wd_ref,
                gfin_ref, o_ref, gate0_ref, gate1_ref, u_ref):
    i = pl.program_id(1)
    last = pl.num_programs(1) - 1
    t = n_ref.shape[1]
    n = n_ref[0]
    left = jnp.where(i == 0, meta_ref[0], left_ref[0])
    right = jnp.where(i == last, jnp.zeros_like(right_ref[0]), right_ref[0])
    n_ext = jnp.concatenate([left, n, right], axis=0)

    for c in range(N_FF_CHUNKS):
        gate_ref = gate1_ref if c % 2 else gate0_ref
        gate_ref[...] = jnp.dot(n_ext, wg_ref[c], preferred_element_type=F32)
        cw = cw_ref[c]
        g = (gate_ref[HALO - 1:HALO - 1 + t, :] * cw[0:1]
             + gate_ref[HALO:HALO + t, :] * cw[1:2]
             + gate_ref[HALO + 1:HALO + 1 + t, :] * cw[2:3]
             + cw[3:4])
        up = jnp.dot(n, wu_ref[c], preferred_element_type=F32)
        act = 0.5 * g * (1.0 + lax.erf(g * (2.0 ** -0.5)))
        u_ref[:, c * FF_CHUNK:(c + 1) * FF_CHUNK] = (act * up).astype(BF16)
    y = h_ref[0] + jnp.dot(u_ref[...], wd_ref[...], preferred_element_type=F32)
    ms = jnp.mean(y * y, axis=-1, keepdims=True)
    o_ref[0] = y * lax.rsqrt(ms + NORM_EPS) * gfin_ref[...]


def _ffn_call(h, n2, n2_meta, wg, wu, cw, wd, g_final, tile):
    b, s, d = h.shape
    grid = (b, s // tile)
    per = tile // HALO
    n_halo_blocks = s // HALO
    return pl.pallas_call(
        _ffn_kernel,
        grid=grid,
        in_specs=[
            pl.BlockSpec((1, tile, d), lambda bi, i: (bi, i, 0)),
            pl.BlockSpec((1, tile, d), lambda bi, i: (bi, i, 0)),
            pl.BlockSpec((1, HALO, d), lambda bi, i: (bi, jnp.maximum(i * per - 1, 0), 0)),
            pl.BlockSpec((1, HALO, d), lambda bi, i: (bi, 0, 0)),
            pl.BlockSpec((1, HALO, d),
                         lambda bi, i: (bi, jnp.minimum((i + 1) * per, n_halo_blocks - 1), 0)),
            _const_spec((N_FF_CHUNKS, d, FF_CHUNK)),
            _const_spec((N_FF_CHUNKS, d, FF_CHUNK)),
            _const_spec((N_FF_CHUNKS, 8, FF_CHUNK)),
            _const_spec((D_FF, d)),
            _const_spec((1, d)),
        ],
        out_specs=pl.BlockSpec((1, tile, d), lambda bi, i: (bi, i, 0)),
        out_shape=jax.ShapeDtypeStruct((b, s, d), F32),
        scratch_shapes=[
            pltpu.VMEM((tile + 2 * HALO, FF_CHUNK), F32),
            pltpu.VMEM((tile + 2 * HALO, FF_CHUNK), F32),
            pltpu.VMEM((tile, D_FF), BF16),
        ],
        compiler_params=_params(2),
        name="ffn",
    )(h, n2, n2, n2_meta, n2, wg, wu, cw, wd, g_final)


def _pair_layout(ang):
    c = jnp.cos(ang)
    s = jnp.sin(ang)
    return jnp.tile(c, (1, 4)), jnp.concatenate([-s, -s, s, s], axis=-1)


def _linear_inv():
    return ROPE_THETA ** (-jnp.arange(0, HEAD_DIM, 2, dtype=F32) / HEAD_DIM)


def _real_tables(s):
    t = jnp.arange(s)
    rowp = (t // GRID_W).astype(F32)
    colp = (t % GRID_W).astype(F32)
    axis_dim = HEAD_DIM // 2
    inv_a = ROPE_THETA ** (-jnp.arange(0, axis_dim, 2, dtype=F32) / axis_dim)
    ang_a = jnp.concatenate([rowp[:, None] * inv_a[None], colp[:, None] * inv_a[None]], axis=-1)
    pos = jnp.arange(N_META + s, dtype=F32)[N_META:]
    ang_b = pos[:, None] * _linear_inv()[None]
    return _pair_layout(ang_a) + _pair_layout(ang_b)


def _meta_tables():
    ang_a = jnp.zeros((META_ROWS, HEAD_DIM // 2), F32)
    pos = jnp.arange(META_ROWS, dtype=F32)
    ang_b = pos[:, None] * _linear_inv()[None]
    return _pair_layout(ang_a) + _pair_layout(ang_b)


def _attn_tiles_per_step(s, tq):
    fixed = 2 * s * 2 * tq * 4 + 2 * 2 * N_KV * s * LANES * 2 + ATTN_TEMP_BYTES
    per_tile = tq * D_MODEL * (2 * 2 * 4 + 2 * 2 * 2 + 2)
    n = 1
    while 2 * n * tq <= s and fixed + 2 * n * per_tile <= VMEM_LIMIT:
        n *= 2
    return n


def _trunk(x, meta_x, meta_qkv, prm, cfg):
    s = x.shape[1]
    q, k, vt = _proj_call(x, _real_tables(s), prm["w_in"], prm["g_mix"], prm["gq"], prm["gk"],
                          prm["bd"], min(s, cfg["proj_tile"]))
    q_m, k_m, vt_m = meta_qkv
    kc = min(s, cfg["key_chunk"])
    tq = min(s, cfg["tq"])
    n_tiles = _attn_tiles_per_step(s, tq)
    h1, n2 = _attn_call(x, q, k, vt, k_m, vt_m, prm["lam"], prm["g_subln"], prm["w_out"],
                        prm["g_ffn"], tq, n_tiles, kc, shared_q=False)
    _, n2_m = _attn_call(meta_x[:, :META_Q], q_m[:, :, :META_Q], k, vt, k_m, vt_m, prm["lam"],
                         prm["g_subln"], prm["w_out"], prm["g_ffn"], META_Q, 1, kc, shared_q=True)
    return _ffn_call(h1, n2, n2_m, prm["wg"], prm["wu"], prm["cw"], prm["wd"], prm["g_final"],
                     min(s, cfg["row_tile"]))


_CFG = dict(proj_tile=512, tq=256, key_chunk=512, row_tile=512)


def kernel(x_prompt, x_sample, meta_tokens, g_mix, w_in, g_qnorm_a, g_knorm_a, lambda_q1, lambda_k1,
           lambda_q2, lambda_k2, g_subln, w_out, g_ffn, w_ff_gate, w_ff_up, conv_w, conv_b,
           w_ff_down, g_final):
    assert w_in.shape[0] == 1, "single-layer trunk"
    d = D_MODEL
    head_order = np.array([0, 4, 1, 5, 2, 6, 3, 7])
    a_cols = (head_order[:, None] * HEAD_DIM + np.arange(HEAD_DIM)[None]).reshape(-1)
    in_perm = np.concatenate([a_cols, np.arange(A_PAIRS * LANES, IN_COLS)])
    out_perm = np.concatenate([a_cols, np.arange(A_PAIRS * LANES, d)])
    half = HEAD_DIM // 2
    tile_perm = np.concatenate([np.arange(0, half), np.arange(2 * half, 3 * half),
                                np.arange(half, 2 * half), np.arange(3 * half, 4 * half)])
    qk_tiles = list(range(0, 5)) + list(range(6, 14))
    for tile in qk_tiles:
        in_perm[tile * LANES:(tile + 1) * LANES] = in_perm[tile * LANES + tile_perm]
    lane_dim = np.concatenate([np.arange(half), np.arange(half),
                               np.arange(half, 2 * half), np.arange(half, 2 * half)])
    blk = (np.arange(LANES) // half) % 2
    cw = jnp.concatenate([conv_w[0], conv_b[0][None], jnp.zeros((4, D_FF), F32)], axis=0)
    lam = jnp.stack([lambda_q1[0], lambda_k1[0], lambda_q2[0], lambda_k2[0]])
    prm = dict(
        w_in=w_in[0][:, in_perm].astype(BF16),
        g_mix=g_mix[0][None],
        gq=g_qnorm_a[0][lane_dim][None],
        gk=g_knorm_a[0][lane_dim][None],
        bd=jnp.asarray((blk[:, None] == blk[None]) / HEAD_DIM, BF16),
        lam=jnp.zeros((8, LANES), F32).at[:4, :HEAD_DIM].set(lam),
        g_subln=g_subln[0][None],
        w_out=w_out[0][out_perm].astype(BF16),
        g_ffn=g_ffn[0][None],
        wg=w_ff_gate[0].reshape(d, N_FF_CHUNKS, FF_CHUNK).transpose(1, 0, 2).astype(BF16),
        wu=w_ff_up[0].reshape(d, N_FF_CHUNKS, FF_CHUNK).transpose(1, 0, 2).astype(BF16),
        cw=cw.reshape(8, N_FF_CHUNKS, FF_CHUNK).transpose(1, 0, 2),
        wd=w_ff_down[0].astype(BF16),
        g_final=g_final[None],
    )
    meta_x = jnp.zeros((1, META_ROWS, d), F32).at[0, :N_META].set(meta_tokens)
    meta_qkv = _proj_call(meta_x, _meta_tables(), prm["w_in"], prm["g_mix"], prm["gq"], prm["gk"],
                          prm["bd"], META_ROWS)
    y_prompt = _trunk(x_prompt, meta_x, meta_qkv, prm, _CFG)
    y_sample = _trunk(x_sample, meta_x, meta_qkv, prm, _CFG)
    return (y_prompt, y_sample)
```

```python
import functools
import math

import jax
import jax.numpy as jnp
import numpy as np
from jax import lax
from jax.experimental import pallas as pl
from jax.experimental.pallas import tpu as pltpu

F32 = jnp.float32
BF16 = jnp.bfloat16

D_MODEL = 1024
HEAD_DIM = 64
N_META = 16
GRID_W = 64
ROPE_THETA = 10000.0
NORM_EPS = 1e-6
LANES = 128
N_PAIRS = 8
N_KV = 5
A_PAIRS = 4
D_FF = 2816
FF_CHUNK = 256
N_FF_CHUNKS = D_FF // FF_CHUNK
IN_COLS = 2304
HALO = 16
META_ROWS = 128
META_Q = 64
LAMBDA_INIT = 0.8 - 0.6 * math.exp(-0.3 * 0)
Q_SCALE = (HEAD_DIM ** -0.5) * math.log2(math.e)
VMEM_LIMIT = 56 * 1024 * 1024
ATTN_TEMP_BYTES = 11 * 1024 * 1024


def _params(n_grid_axes):
    return pltpu.CompilerParams(
        dimension_semantics=("arbitrary",) * n_grid_axes,
        vmem_limit_bytes=VMEM_LIMIT,
    )


def _const_spec(shape):
    zeros = (0,) * len(shape)
    return pl.BlockSpec(shape, lambda *_: zeros, pipeline_mode=pl.Buffered(1))


def _proj_kernel(x_ref, gmix_ref, w_ref, gq_ref, gk_ref, bd_ref, ca_ref, sa_ref, cb_ref, sb_ref,
                 q_ref, k_ref, vt_ref):
    x = x_ref[0]
    t = x.shape[0]
    ms = jnp.mean(x * x, axis=-1, keepdims=True)
    n = (x * lax.rsqrt(ms + NORM_EPS) * gmix_ref[...]).astype(BF16)

    def rope(y, c, s):
        return y * c + pltpu.roll(y, LANES // 2, 1) * s

    def proj(col):
        y = jnp.dot(n, w_ref[:, col:col + 2 * LANES], preferred_element_type=F32)
        return y[:, :LANES], y[:, LANES:]

    ca, sa, cb, sb = ca_ref[...], sa_ref[...], cb_ref[...], sb_ref[...]
    gq, gk = gq_ref[...], gk_ref[...]

    ya = list(proj(0) + proj(2 * LANES))
    yk, yv = proj(4 * LANES)
    vt_ref[0, 0] = yv.T.astype(BF16)
    sq = jnp.concatenate([(y * y).astype(BF16) for y in ya + [yk]], axis=0)
    ms_all = jnp.dot(sq, bd_ref[...], preferred_element_type=F32)
    for p in range(A_PAIRS):
        r = lax.rsqrt(ms_all[p * t:(p + 1) * t] + NORM_EPS) * Q_SCALE
        q_ref[0, p] = (rope(ya[p] * gq, ca, sa) * r).astype(BF16)
    r = lax.rsqrt(ms_all[A_PAIRS * t:] + NORM_EPS)
    k_ref[0, 0] = (rope(yk * gk, ca, sa) * r).astype(BF16)
    for j in range(2):
        y0, y1 = proj(14 * LANES + 2 * LANES * j)
        vt_ref[0, 1 + 2 * j] = y0.T.astype(BF16)
        vt_ref[0, 2 + 2 * j] = y1.T.astype(BF16)
    for j in range(2):
        y0, y1 = proj(10 * LANES + 2 * LANES * j)
        k_ref[0, 1 + 2 * j] = rope(y0, cb, sb).astype(BF16)
        k_ref[0, 2 + 2 * j] = rope(y1, cb, sb).astype(BF16)
    for j in range(2):
        y0, y1 = proj(6 * LANES + 2 * LANES * j)
        q_ref[0, A_PAIRS + 2 * j] = (rope(y0, cb, sb) * Q_SCALE).astype(BF16)
        q_ref[0, A_PAIRS + 2 * j + 1] = (rope(y1, cb, sb) * Q_SCALE).astype(BF16)


def _proj_call(x, tables, w_in, gmix, gq, gk, bd, tile):
    b, s, d = x.shape
    ca, sa, cb, sb = tables
    grid = (b, s // tile)
    tab_spec = pl.BlockSpec((tile, LANES), lambda bi, i: (i, 0))
    return pl.pallas_call(
        _proj_kernel,
        grid=grid,
        in_specs=[
            pl.BlockSpec((1, tile, d), lambda bi, i: (bi, i, 0)),
            _const_spec((1, d)),
            _const_spec((d, IN_COLS)),
            _const_spec((1, LANES)),
            _const_spec((1, LANES)),
            _const_spec((LANES, LANES)),
            tab_spec, tab_spec, tab_spec, tab_spec,
        ],
        out_specs=[
            pl.BlockSpec((1, N_PAIRS, tile, LANES), lambda bi, i: (bi, 0, i, 0)),
            pl.BlockSpec((1, N_KV, tile, LANES), lambda bi, i: (bi, 0, i, 0)),
            pl.BlockSpec((1, N_KV, LANES, tile), lambda bi, i: (bi, 0, 0, i)),
        ],
        out_shape=[
            jax.ShapeDtypeStruct((b, N_PAIRS, s, LANES), BF16),
            jax.ShapeDtypeStruct((b, N_KV, s, LANES), BF16),
            jax.ShapeDtypeStruct((b, N_KV, LANES, s), BF16),
        ],
        compiler_params=_params(2),
        name="proj",
    )(x, gmix, w_in, gq, gk, bd, ca, sa, cb, sb)


def _attn_kernel(x_ref, q_ref, k_ref, vt_ref, km_ref, vtm_ref, lam_ref, gsub_ref, wo_ref, gffn_ref,
                 h_ref, n_ref, sa_ref, sb_ref, acca_ref, accb_ref, la_ref, lb_ref, mix_ref,
                 *, tq, key_chunk):
    n_tiles = q_ref.shape[2] // tq
    s_len = k_ref.shape[2]
    n_chunks = s_len // key_chunk
    lane = lax.broadcasted_iota(jnp.int32, (tq, LANES), 1)
    q_low = (lane & (HEAD_DIM // 2)) == 0
    low = lane < HEAD_DIM
    contract_lanes = (((1,), (1,)), ((), ()))

    lam_p = lam_ref[...]
    lam = (jnp.exp(jnp.sum(lam_p[0:1] * lam_p[1:2], axis=-1, keepdims=True))
           - jnp.exp(jnp.sum(lam_p[2:3] * lam_p[3:4], axis=-1, keepdims=True))
           + LAMBDA_INIT)

    def chunk(c):
        return slice(c * key_chunk, (c + 1) * key_chunk)

    def locate(is_b, i):
        if isinstance(i, int):
            t, j = divmod(i, A_PAIRS)
        else:
            t, j = lax.shift_right_logical(i, 2), lax.bitwise_and(i, A_PAIRS - 1)
        return t, (j + A_PAIRS if is_b else j), (j + 1 if is_b else 0)

    def stage(scores_job=None, softmax_job=None, finish_job=None):
        if scores_job is not None:
            b1, i1, buf1 = scores_job
            t1, p1, kv1 = locate(b1, i1)
            row0 = t1 * tq if isinstance(t1, int) else pl.multiple_of(t1 * tq, tq)
            qp = q_ref[0, p1, pl.ds(row0, tq), :].astype(F32)
            qq = jnp.concatenate([jnp.where(q_low, qp, 0.0), jnp.where(q_low, 0.0, qp)],
                                 axis=0).astype(BF16)
            sm1 = lax.dot_general(km_ref[0, kv1][:N_META], qq, contract_lanes,
                                  preferred_element_type=F32)
            m8 = jnp.maximum(sm1[:8], sm1[8:])
        if softmax_job is not None:
            b2, i2, buf2, (sm2, m2) = softmax_job
            _, _, kv2 = locate(b2, i2)
            pm = jnp.exp2(sm2 - m2)
            l8 = pm[:8] + pm[8:]
            pm_pad = jnp.concatenate(
                [pm.astype(BF16), jnp.zeros((META_ROWS - N_META, 2 * tq), BF16)], axis=0)
            acc = jnp.dot(vtm_ref[0, kv2], pm_pad, preferred_element_type=F32)
        out = None
        if finish_job is not None:
            b3, i3 = finish_job
            t3, p3, _ = locate(b3, i3)
            acc_ref, l_ref = (accb_ref, lb_ref) if b3 else (acca_ref, la_ref)
            inv_l = 1.0 / jnp.sum(l_ref[...], axis=0, keepdims=True)
            ot = (acc_ref[...] * inv_l).T
            o_lo, o_hi = ot[:tq], ot[tq:]
            if b3:
                dt = o_lo - lam * o_hi
                msd = jnp.mean(dt * dt, axis=-1, keepdims=True)
                out = (dt * lax.rsqrt(msd + NORM_EPS) * gsub_ref[...]
                       * (1.0 - LAMBDA_INIT)).astype(BF16)
            else:
                out = jnp.where(low, o_lo, o_hi).astype(BF16)
            mix_ref[t3, p3] = out
        for c in range(n_chunks):
            if scores_job is not None:
                sc = lax.dot_general(k_ref[0, kv1, chunk(c), :], qq, contract_lanes,
                                     preferred_element_type=F32)
                buf1[chunk(c), :] = sc
                m8 = jnp.maximum(m8, jnp.max(sc.reshape(key_chunk // 8, 8, 2 * tq), axis=0))
            if softmax_job is not None:
                pc = jnp.exp2(buf2[chunk(c), :] - m2)
                l8 = l8 + jnp.sum(pc.reshape(key_chunk // 8, 8, 2 * tq), axis=0)
                acc = acc + jnp.dot(vt_ref[0, kv2, :, chunk(c)], pc.astype(BF16),
                                    preferred_element_type=F32)
        if softmax_job is not None:
            acc_ref, l_ref = (accb_ref, lb_ref) if b2 else (acca_ref, la_ref)
            acc_ref[...] = acc
            l_ref[...] = l8
        if scores_job is not None:
            return sm1, jnp.max(m8, axis=0, keepdims=True)
        return out

    grp_a, grp_b = False, True
    last = n_tiles * A_PAIRS - 1
    stats = stage(scores_job=(grp_a, 0, sa_ref))
    stats = stage(scores_job=(grp_b, 0, sb_ref), softmax_job=(grp_a, 0, sa_ref, stats))
    stats = stage(scores_job=(grp_a, 1, sa_ref), softmax_job=(grp_b, 0, sb_ref, stats),
                  finish_job=(grp_a, 0))

    def body(i, stats_a):
        stats_b = stage(scores_job=(grp_b, i, sb_ref), softmax_job=(grp_a, i, sa_ref, stats_a),
                        finish_job=(grp_b, i - 1))
        return stage(scores_job=(grp_a, i + 1, sa_ref), softmax_job=(grp_b, i, sb_ref, stats_b),
                     finish_job=(grp_a, i))

    stats = lax.fori_loop(1, last, body, stats)
    stats = stage(scores_job=(grp_b, last, sb_ref), softmax_job=(grp_a, last, sa_ref, stats),
                  finish_job=(grp_b, last - 1))
    stage(softmax_job=(grp_b, last, sb_ref, stats), finish_job=(grp_a, last))
    out_last = stage(finish_job=(grp_b, last))

    split = (N_PAIRS - 1) * LANES
    for t in range(n_tiles):
        rows = slice(t * tq, (t + 1) * tq)
        if t + 1 < n_tiles:
            mix = jnp.concatenate([mix_ref[t, p] for p in range(N_PAIRS)], axis=1)
            h = x_ref[0, rows, :] + jnp.dot(mix, wo_ref[...], preferred_element_type=F32)
        else:
            mix = jnp.concatenate([mix_ref[t, p] for p in range(N_PAIRS - 1)], axis=1)
            h = (x_ref[0, rows, :]
                 + jnp.dot(mix, wo_ref[:split, :], preferred_element_type=F32)
                 + jnp.dot(out_last, wo_ref[split:, :], preferred_element_type=F32))
        h_ref[0, rows, :] = h
        ms = jnp.mean(h * h, axis=-1, keepdims=True)
        n_ref[0, rows, :] = (h * lax.rsqrt(ms + NORM_EPS) * gffn_ref[...]).astype(BF16)


def _attn_call(x, q, k, vt, km, vtm, lam_p, gsub, w_out, g_ffn, tq, n_tiles, key_chunk, shared_q):
    _, _, sq, _ = q.shape
    b, _, s, _ = k.shape
    d = x.shape[-1]
    rows = tq * n_tiles
    grid = (b, sq // rows)
    if shared_q:
        q_map = lambda bi, i: (0, 0, i, 0)
        x_map = lambda bi, i: (0, i, 0)
    else:
        q_map = lambda bi, i: (bi, 0, i, 0)
        x_map = lambda bi, i: (bi, i, 0)
    return pl.pallas_call(
        functools.partial(_attn_kernel, tq=tq, key_chunk=key_chunk),
        grid=grid,
        in_specs=[
            pl.BlockSpec((1, rows, d), x_map),
            pl.BlockSpec((1, N_PAIRS, rows, LANES), q_map),
            pl.BlockSpec((1, N_KV, s, LANES), lambda bi, i: (bi, 0, 0, 0)),
            pl.BlockSpec((1, N_KV, LANES, s), lambda bi, i: (bi, 0, 0, 0)),
            _const_spec((1, N_KV, META_ROWS, LANES)),
            _const_spec((1, N_KV, LANES, META_ROWS)),
            _const_spec((8, LANES)),
            _const_spec((1, LANES)),
            _const_spec((d, d)),
            _const_spec((1, d)),
        ],
        out_specs=[
            pl.BlockSpec((1, rows, d), lambda bi, i: (bi, i, 0)),
            pl.BlockSpec((1, rows, d), lambda bi, i: (bi, i, 0)),
        ],
        out_shape=[
            jax.ShapeDtypeStruct((b, sq, d), F32),
            jax.ShapeDtypeStruct((b, sq, d), BF16),
        ],
        scratch_shapes=[
            pltpu.VMEM((s, 2 * tq), F32),
            pltpu.VMEM((s, 2 * tq), F32),
            pltpu.VMEM((LANES, 2 * tq), F32),
            pltpu.VMEM((LANES, 2 * tq), F32),
            pltpu.VMEM((8, 2 * tq), F32),
            pltpu.VMEM((8, 2 * tq), F32),
            pltpu.VMEM((n_tiles, N_PAIRS, tq, LANES), BF16),
        ],
        compiler_params=_params(2),
        name="attn",
    )(x, q, k, vt, km, vtm, lam_p, gsub, w_out, g_ffn)


def _ffn_kernel(h_ref, n_ref, left_ref, meta_ref, right_ref, wg_ref, wu_ref, cw_ref, wd_ref,
                gfin_ref, o_ref, gate0_ref, gate1_ref, u_ref):
    i = pl.program_id(1)
    last = pl.num_programs(1) - 1
    t = n_ref.shape[1]
    n = n_ref[0]
    left = jnp.where(i == 0, meta_ref[0], left_ref[0])
    right = jnp.where(i == last, jnp.zeros_like(right_ref[0]), right_ref[0])
    n_ext = jnp.concatenate([left, n, right], axis=0)

    for c in range(N_FF_CHUNKS):
        gate_ref = gate1_ref if c % 2 else gate0_ref
        cols = slice(c * FF_CHUNK, (c + 1) * FF_CHUNK)
        gate_ref[...] = jnp.dot(n_ext, wg_ref[:, cols], preferred_element_type=F32)
        cw = cw_ref[:, cols]
        g = (gate_ref[HALO - 1:HALO - 1 + t, :] * cw[0:1]
             + gate_ref[HALO:HALO + t, :] * cw[1:2]
             + gate_ref[HALO + 1:HALO + 1 + t, :] * cw[2:3]
             + cw[3:4])
        up = jnp.dot(n, wu_ref[:, cols], preferred_element_type=F32)
        act = 0.5 * g * (1.0 + lax.erf(g * (2.0 ** -0.5)))
        u_ref[:, cols] = (act * up).astype(BF16)
    y = h_ref[0] + jnp.dot(u_ref[...], wd_ref[...], preferred_element_type=F32)
    ms = jnp.mean(y * y, axis=-1, keepdims=True)
    o_ref[0] = y * lax.rsqrt(ms + NORM_EPS) * gfin_ref[...]


def _ffn_call(h, n2, n2_meta, wg, wu, cw, wd, g_final, tile):
    b, s, d = h.shape
    grid = (b, s // tile)
    per = tile // HALO
    n_halo_blocks = s // HALO
    return pl.pallas_call(
        _ffn_kernel,
        grid=grid,
        in_specs=[
            pl.BlockSpec((1, tile, d), lambda bi, i: (bi, i, 0)),
            pl.BlockSpec((1, tile, d), lambda bi, i: (bi, i, 0)),
            pl.BlockSpec((1, HALO, d), lambda bi, i: (bi, jnp.maximum(i * per - 1, 0), 0)),
            pl.BlockSpec((1, HALO, d), lambda bi, i: (bi, 0, 0)),
            pl.BlockSpec((1, HALO, d),
                         lambda bi, i: (bi, jnp.minimum((i + 1) * per, n_halo_blocks - 1), 0)),
            _const_spec((d, D_FF)),
            _const_spec((d, D_FF)),
            _const_spec((8, D_FF)),
            _const_spec((D_FF, d)),
            _const_spec((1, d)),
        ],
        out_specs=pl.BlockSpec((1, tile, d), lambda bi, i: (bi, i, 0)),
        out_shape=jax.ShapeDtypeStruct((b, s, d), F32),
        scratch_shapes=[
            pltpu.VMEM((tile + 2 * HALO, FF_CHUNK), F32),
            pltpu.VMEM((tile + 2 * HALO, FF_CHUNK), F32),
            pltpu.VMEM((tile, D_FF), BF16),
        ],
        compiler_params=_params(2),
        name="ffn",
    )(h, n2, n2, n2_meta, n2, wg, wu, cw, wd, g_final)


def _pair_layout(ang):
    c = jnp.cos(ang)
    s = jnp.sin(ang)
    return jnp.tile(c, (1, 4)), jnp.concatenate([-s, -s, s, s], axis=-1)


def _linear_inv():
    return ROPE_THETA ** (-jnp.arange(0, HEAD_DIM, 2, dtype=F32) / HEAD_DIM)


def _real_tables(s):
    t = jnp.arange(s)
    rowp = (t // GRID_W).astype(F32)
    colp = (t % GRID_W).astype(F32)
    axis_dim = HEAD_DIM // 2
    inv_a = ROPE_THETA ** (-jnp.arange(0, axis_dim, 2, dtype=F32) / axis_dim)
    ang_a = jnp.concatenate([rowp[:, None] * inv_a[None], colp[:, None] * inv_a[None]], axis=-1)
    pos = jnp.arange(N_META + s, dtype=F32)[N_META:]
    ang_b = pos[:, None] * _linear_inv()[None]
    return _pair_layout(ang_a) + _pair_layout(ang_b)


def _meta_tables():
    ang_a = jnp.zeros((META_ROWS, HEAD_DIM // 2), F32)
    pos = jnp.arange(META_ROWS, dtype=F32)
    ang_b = pos[:, None] * _linear_inv()[None]
    return _pair_layout(ang_a) + _pair_layout(ang_b)


def _attn_tiles_per_step(s, tq):
    fixed = 2 * s * 2 * tq * 4 + 2 * 2 * N_KV * s * LANES * 2 + ATTN_TEMP_BYTES
    per_tile = tq * D_MODEL * (2 * 2 * 4 + 2 * 2 * 2 + 2)
    n = 1
    while 2 * n * tq <= s and fixed + 2 * n * per_tile <= VMEM_LIMIT:
        n *= 2
    return n


def _trunk(x, meta_x, meta_qkv, prm, cfg):
    s = x.shape[1]
    q, k, vt = _proj_call(x, _real_tables(s), prm["w_in"], prm["g_mix"], prm["gq"], prm["gk"],
                          prm["bd"], min(s, cfg["proj_tile"]))
    q_m, k_m, vt_m = meta_qkv
    kc = min(s, cfg["key_chunk"])
    tq = min(s, cfg["tq"])
    n_tiles = _attn_tiles_per_step(s, tq)
    h1, n2 = _attn_call(x, q, k, vt, k_m, vt_m, prm["lam"], prm["g_subln"], prm["w_out"],
                        prm["g_ffn"], tq, n_tiles, kc, shared_q=False)
    _, n2_m = _attn_call(meta_x[:, :META_Q], q_m[:, :, :META_Q], k, vt, k_m, vt_m, prm["lam"],
                         prm["g_subln"], prm["w_out"], prm["g_ffn"], META_Q, 1, kc, shared_q=True)
    return _ffn_call(h1, n2, n2_m, prm["wg"], prm["wu"], prm["cw"], prm["wd"], prm["g_final"],
                     min(s, cfg["row_tile"]))


_CFG = dict(proj_tile=1024, tq=256, key_chunk=512, row_tile=1024)


def kernel(x_prompt, x_sample, meta_tokens, g_mix, w_in, g_qnorm_a, g_knorm_a, lambda_q1, lambda_k1,
           lambda_q2, lambda_k2, g_subln, w_out, g_ffn, w_ff_gate, w_ff_up, conv_w, conv_b,
           w_ff_down, g_final):
    assert w_in.shape[0] == 1, "single-layer trunk"
    d = D_MODEL
    head_order = np.array([0, 4, 1, 5, 2, 6, 3, 7])
    a_cols = (head_order[:, None] * HEAD_DIM + np.arange(HEAD_DIM)[None]).reshape(-1)
    in_perm = np.concatenate([a_cols, np.arange(A_PAIRS * LANES, IN_COLS)])
    out_perm = np.concatenate([a_cols, np.arange(A_PAIRS * LANES, d)])
    half = HEAD_DIM // 2
    tile_perm = np.concatenate([np.arange(0, half), np.arange(2 * half, 3 * half),
                                np.arange(half, 2 * half), np.arange(3 * half, 4 * half)])
    qk_tiles = list(range(0, 5)) + list(range(6, 14))
    for tile in qk_tiles:
        in_perm[tile * LANES:(tile + 1) * LANES] = in_perm[tile * LANES + tile_perm]
    lane_dim = np.concatenate([np.arange(half), np.arange(half),
                               np.arange(half, 2 * half), np.arange(half, 2 * half)])
    blk = (np.arange(LANES) // half) % 2
    cw = jnp.concatenate([conv_w[0], conv_b[0][None], jnp.zeros((4, D_FF), F32)], axis=0)
    lam = jnp.stack([lambda_q1[0], lambda_k1[0], lambda_q2[0], lambda_k2[0]])
    prm = dict(
        w_in=w_in[0][:, in_perm].astype(BF16),
        g_mix=g_mix[0][None],
        gq=g_qnorm_a[0][lane_dim][None],
        gk=g_knorm_a[0][lane_dim][None],
        bd=jnp.asarray((blk[:, None] == blk[None]) / HEAD_DIM, BF16),
        lam=jnp.zeros((8, LANES), F32).at[:4, :HEAD_DIM].set(lam),
        g_subln=g_subln[0][None],
        w_out=w_out[0][out_perm].astype(BF16),
        g_ffn=g_ffn[0][None],
        wg=w_ff_gate[0].astype(BF16),
        wu=w_ff_up[0].astype(BF16),
        cw=cw,
        wd=w_ff_down[0].astype(BF16),
        g_final=g_final[None],
    )
    meta_x = jnp.zeros((1, META_ROWS, d), F32).at[0, :N_META].set(meta_tokens)
    meta_qkv = _proj_call(meta_x, _meta_tables(), prm["w_in"], prm["g_mix"], prm["gq"], prm["gk"],
                          prm["bd"], META_ROWS)
    y_prompt = _trunk(x_prompt, meta_x, meta_qkv, prm, _CFG)
    y_sample = _trunk(x_sample, meta_x, meta_qkv, prm, _CFG)
    return (y_prompt, y_sample)
```

```python
import functools
import math

import jax
import jax.numpy as jnp
import numpy as np
from jax import lax
from jax.experimental import pallas as pl
from jax.experimental.pallas import tpu as pltpu

F32 = jnp.float32
BF16 = jnp.bfloat16

D_MODEL = 1024
HEAD_DIM = 64
N_META = 16
GRID_W = 64
ROPE_THETA = 10000.0
NORM_EPS = 1e-6
LANES = 128
N_PAIRS = 8
N_KV = 5
A_PAIRS = 4
D_FF = 2816
FF_CHUNK = 256
N_FF_CHUNKS = D_FF // FF_CHUNK
IN_COLS = 2304
HALO = 16
META_ROWS = 128
META_Q = 64
LAMBDA_INIT = 0.8 - 0.6 * math.exp(-0.3 * 0)
Q_SCALE = (HEAD_DIM ** -0.5) * math.log2(math.e)
VMEM_LIMIT = 56 * 1024 * 1024
ATTN_TEMP_BYTES = 11 * 1024 * 1024


def _params(n_grid_axes):
    return pltpu.CompilerParams(
        dimension_semantics=("arbitrary",) * n_grid_axes,
        vmem_limit_bytes=VMEM_LIMIT,
    )


def _const_spec(shape):
    zeros = (0,) * len(shape)
    return pl.BlockSpec(shape, lambda *_: zeros, pipeline_mode=pl.Buffered(1))


def _proj_kernel(x_ref, gmix_ref, w_ref, gq_ref, gk_ref, bd_ref, ca_ref, sa_ref, cb_ref, sb_ref,
                 q_ref, k_ref, vt_ref):
    x = x_ref[0]
    t = x.shape[0]
    ms = jnp.mean(x * x, axis=-1, keepdims=True)
    n = (x * lax.rsqrt(ms + NORM_EPS) * gmix_ref[...]).astype(BF16)

    def rope(y, c, s):
        return y * c + pltpu.roll(y, LANES // 2, 1) * s

    def proj(col):
        y = jnp.dot(n, w_ref[:, col:col + 2 * LANES], preferred_element_type=F32)
        return y[:, :LANES], y[:, LANES:]

    ca, sa, cb, sb = ca_ref[...], sa_ref[...], cb_ref[...], sb_ref[...]
    gq, gk = gq_ref[...], gk_ref[...]

    ya = list(proj(0) + proj(2 * LANES))
    yk, yv = proj(4 * LANES)
    vt_ref[0, 0] = yv.T.astype(BF16)
    sq = jnp.concatenate([(y * y).astype(BF16) for y in ya + [yk]], axis=0)
    ms_all = jnp.dot(sq, bd_ref[...], preferred_element_type=F32)
    for p in range(A_PAIRS):
        r = lax.rsqrt(ms_all[p * t:(p + 1) * t] + NORM_EPS) * Q_SCALE
        q_ref[0, p] = (rope(ya[p] * gq, ca, sa) * r).astype(BF16)
    r = lax.rsqrt(ms_all[A_PAIRS * t:] + NORM_EPS)
    k_ref[0, 0] = (rope(yk * gk, ca, sa) * r).astype(BF16)
    for j in range(2):
        y0, y1 = proj(14 * LANES + 2 * LANES * j)
        vt_ref[0, 1 + 2 * j] = y0.T.astype(BF16)
        vt_ref[0, 2 + 2 * j] = y1.T.astype(BF16)
    for j in range(2):
        y0, y1 = proj(10 * LANES + 2 * LANES * j)
        k_ref[0, 1 + 2 * j] = rope(y0, cb, sb).astype(BF16)
        k_ref[0, 2 + 2 * j] = rope(y1, cb, sb).astype(BF16)
    for j in range(2):
        y0, y1 = proj(6 * LANES + 2 * LANES * j)
        q_ref[0, A_PAIRS + 2 * j] = (rope(y0, cb, sb) * Q_SCALE).astype(BF16)
        q_ref[0, A_PAIRS + 2 * j + 1] = (rope(y1, cb, sb) * Q_SCALE).astype(BF16)


def _proj_call(x, tables, w_in, gmix, gq, gk, bd, tile):
    b, s, d = x.shape
    ca, sa, cb, sb = tables
    grid = (b, s // tile)
    tab_spec = pl.BlockSpec((tile, LANES), lambda bi, i: (i, 0))
    return pl.pallas_call(
        _proj_kernel,
        grid=grid,
        in_specs=[
            pl.BlockSpec((1, tile, d), lambda bi, i: (bi, i, 0)),
            _const_spec((1, d)),
            _const_spec((d, IN_COLS)),
            _const_spec((1, LANES)),
            _const_spec((1, LANES)),
            _const_spec((LANES, LANES)),
            tab_spec, tab_spec, tab_spec, tab_spec,
        ],
        out_specs=[
            pl.BlockSpec((1, N_PAIRS, tile, LANES), lambda bi, i: (bi, 0, i, 0)),
            pl.BlockSpec((1, N_KV, tile, LANES), lambda bi, i: (bi, 0, i, 0)),
            pl.BlockSpec((1, N_KV, LANES, tile), lambda bi, i: (bi, 0, 0, i)),
        ],
        out_shape=[
            jax.ShapeDtypeStruct((b, N_PAIRS, s, LANES), BF16),
            jax.ShapeDtypeStruct((b, N_KV, s, LANES), BF16),
            jax.ShapeDtypeStruct((b, N_KV, LANES, s), BF16),
        ],
        compiler_params=_params(2),
        name="proj",
    )(x, gmix, w_in, gq, gk, bd, ca, sa, cb, sb)


def _attn_kernel(x_ref, q_ref, k_ref, vt_ref, km_ref, vtm_ref, lam_ref, gsub_ref, wo_ref, gffn_ref,
                 h_ref, n_ref, sa_ref, sb_ref, acca_ref, accb_ref, la_ref, lb_ref, mix_ref,
                 *, tq, key_chunk):
    n_tiles = q_ref.shape[2] // tq
    s_len = k_ref.shape[2]
    n_chunks = s_len // key_chunk
    lane = lax.broadcasted_iota(jnp.int32, (tq, LANES), 1)
    q_low = (lane & (HEAD_DIM // 2)) == 0
    low = lane < HEAD_DIM
    contract_lanes = (((1,), (1,)), ((), ()))

    lam_p = lam_ref[...]
    lam = (jnp.exp(jnp.sum(lam_p[0:1] * lam_p[1:2], axis=-1, keepdims=True))
           - jnp.exp(jnp.sum(lam_p[2:3] * lam_p[3:4], axis=-1, keepdims=True))
           + LAMBDA_INIT)

    def chunk(c):
        return slice(c * key_chunk, (c + 1) * key_chunk)

    def locate(is_b, i):
        if isinstance(i, int):
            t, j = divmod(i, A_PAIRS)
        else:
            t, j = lax.shift_right_logical(i, 2), lax.bitwise_and(i, A_PAIRS - 1)
        return t, (j + A_PAIRS if is_b else j), (j + 1 if is_b else 0)

    def stage(scores_job=None, softmax_job=None, finish_job=None):
        if scores_job is not None:
            b1, i1, buf1 = scores_job
            t1, p1, kv1 = locate(b1, i1)
            row0 = t1 * tq if isinstance(t1, int) else pl.multiple_of(t1 * tq, tq)
            qp = q_ref[0, p1, pl.ds(row0, tq), :].astype(F32)
            qq = jnp.concatenate([jnp.where(q_low, qp, 0.0), jnp.where(q_low, 0.0, qp)],
                                 axis=0).astype(BF16)
            sm1 = lax.dot_general(km_ref[0, kv1][:N_META], qq, contract_lanes,
                                  preferred_element_type=F32)
            m8 = jnp.maximum(sm1[:8], sm1[8:])
        if softmax_job is not None:
            b2, i2, buf2, (sm2, m2) = softmax_job
            _, _, kv2 = locate(b2, i2)
            pm = jnp.exp2(sm2 - m2)
            l8 = pm[:8] + pm[8:]
            pm_pad = jnp.concatenate(
                [pm.astype(BF16), jnp.zeros((META_ROWS - N_META, 2 * tq), BF16)], axis=0)
            acc = jnp.dot(vtm_ref[0, kv2], pm_pad, preferred_element_type=F32)
        out = None
        if finish_job is not None:
            b3, i3 = finish_job
            t3, p3, _ = locate(b3, i3)
            acc_ref, l_ref = (accb_ref, lb_ref) if b3 else (acca_ref, la_ref)
            inv_l = 1.0 / jnp.sum(l_ref[...], axis=0, keepdims=True)
            ot = (acc_ref[...] * inv_l).T
            o_lo, o_hi = ot[:tq], ot[tq:]
            if b3:
                dt = o_lo - lam * o_hi
                msd = jnp.mean(dt * dt, axis=-1, keepdims=True)
                out = (dt * lax.rsqrt(msd + NORM_EPS) * gsub_ref[...]
                       * (1.0 - LAMBDA_INIT)).astype(BF16)
            else:
                out = jnp.where(low, o_lo, o_hi).astype(BF16)
            mix_ref[t3, p3] = out
        for c in range(n_chunks):
            if scores_job is not None:
                sc = lax.dot_general(k_ref[0, kv1, chunk(c), :], qq, contract_lanes,
                                     preferred_element_type=F32)
                buf1[chunk(c), :] = sc
                m8 = jnp.maximum(m8, jnp.max(sc.reshape(key_chunk // 8, 8, 2 * tq), axis=0))
            if softmax_job is not None:
                pc = jnp.exp2(buf2[chunk(c), :] - m2)
                l8 = l8 + jnp.sum(pc.reshape(key_chunk // 8, 8, 2 * tq), axis=0)
                acc = acc + jnp.dot(vt_ref[0, kv2, :, chunk(c)], pc.astype(BF16),
                                    preferred_element_type=F32)
        if softmax_job is not None:
            acc_ref, l_ref = (accb_ref, lb_ref) if b2 else (acca_ref, la_ref)
            acc_ref[...] = acc
            l_ref[...] = l8
        if scores_job is not None:
            return sm1, jnp.max(m8, axis=0, keepdims=True)
        return out

    grp_a, grp_b = False, True
    last = n_tiles * A_PAIRS - 1
    stats = stage(scores_job=(grp_a, 0, sa_ref))
    stats = stage(scores_job=(grp_b, 0, sb_ref), softmax_job=(grp_a, 0, sa_ref, stats))
    stats = stage(scores_job=(grp_a, 1, sa_ref), softmax_job=(grp_b, 0, sb_ref, stats),
                  finish_job=(grp_a, 0))

    def body(i, stats_a):
        stats_b = stage(scores_job=(grp_b, i, sb_ref), softmax_job=(grp_a, i, sa_ref, stats_a),
                        finish_job=(grp_b, i - 1))
        return stage(scores_job=(grp_a, i + 1, sa_ref), softmax_job=(grp_b, i, sb_ref, stats_b),
                     finish_job=(grp_a, i))

    stats = lax.fori_loop(1, last, body, stats)
    stats = stage(scores_job=(grp_b, last, sb_ref), softmax_job=(grp_a, last, sa_ref, stats),
                  finish_job=(grp_b, last - 1))
    stage(softmax_job=(grp_b, last, sb_ref, stats), finish_job=(grp_a, last))
    out_last = stage(finish_job=(grp_b, last))

    split = (N_PAIRS - 1) * LANES
    for t in range(n_tiles):
        rows = slice(t * tq, (t + 1) * tq)
        if t + 1 < n_tiles:
            mix = jnp.concatenate([mix_ref[t, p] for p in range(N_PAIRS)], axis=1)
            h = x_ref[0, rows, :] + jnp.dot(mix, wo_ref[...], preferred_element_type=F32)
        else:
            mix = jnp.concatenate([mix_ref[t, p] for p in range(N_PAIRS - 1)], axis=1)
            h = (x_ref[0, rows, :]
                 + jnp.dot(mix, wo_ref[:split, :], preferred_element_type=F32)
                 + jnp.dot(out_last, wo_ref[split:, :], preferred_element_type=F32))
        h_ref[0, rows, :] = h
        ms = jnp.mean(h * h, axis=-1, keepdims=True)
        n_ref[0, rows, :] = (h * lax.rsqrt(ms + NORM_EPS) * gffn_ref[...]).astype(BF16)


def _attn_call(x, q, k, vt, km, vtm, lam_p, gsub, w_out, g_ffn, tq, n_tiles, vt_buffers, key_chunk,
               shared_q):
    _, _, sq, _ = q.shape
    b, _, s, _ = k.shape
    d = x.shape[-1]
    rows = tq * n_tiles
    grid = (b, sq // rows)
    if shared_q:
        q_map = lambda bi, i: (0, 0, i, 0)
        x_map = lambda bi, i: (0, i, 0)
    else:
        q_map = lambda bi, i: (bi, 0, i, 0)
        x_map = lambda bi, i: (bi, i, 0)
    return pl.pallas_call(
        functools.partial(_attn_kernel, tq=tq, key_chunk=key_chunk),
        grid=grid,
        in_specs=[
            pl.BlockSpec((1, rows, d), x_map),
            pl.BlockSpec((1, N_PAIRS, rows, LANES), q_map),
            pl.BlockSpec((1, N_KV, s, LANES), lambda bi, i: (bi, 0, 0, 0)),
            pl.BlockSpec((1, N_KV, LANES, s), lambda bi, i: (bi, 0, 0, 0),
                         pipeline_mode=pl.Buffered(vt_buffers)),
            _const_spec((1, N_KV, META_ROWS, LANES)),
            _const_spec((1, N_KV, LANES, META_ROWS)),
            _const_spec((8, LANES)),
            _const_spec((1, LANES)),
            _const_spec((d, d)),
            _const_spec((1, d)),
        ],
        out_specs=[
            pl.BlockSpec((1, rows, d), lambda bi, i: (bi, i, 0)),
            pl.BlockSpec((1, rows, d), lambda bi, i: (bi, i, 0)),
        ],
        out_shape=[
            jax.ShapeDtypeStruct((b, sq, d), F32),
            jax.ShapeDtypeStruct((b, sq, d), BF16),
        ],
        scratch_shapes=[
            pltpu.VMEM((s, 2 * tq), F32),
            pltpu.VMEM((s, 2 * tq), F32),
            pltpu.VMEM((LANES, 2 * tq), F32),
            pltpu.VMEM((LANES, 2 * tq), F32),
            pltpu.VMEM((8, 2 * tq), F32),
            pltpu.VMEM((8, 2 * tq), F32),
            pltpu.VMEM((n_tiles, N_PAIRS, tq, LANES), BF16),
        ],
        compiler_params=_params(2),
        name="attn",
    )(x, q, k, vt, km, vtm, lam_p, gsub, w_out, g_ffn)


def _ffn_kernel(h_ref, n_ref, left_ref, meta_ref, right_ref, wg_ref, wu_ref, cw_ref, wd_ref,
                gfin_ref, o_ref, gate0_ref, gate1_ref, u_ref):
    i = pl.program_id(1)
    last = pl.num_programs(1) - 1
    t = n_ref.shape[1]
    n = n_ref[0]
    left = jnp.where(i == 0, meta_ref[0], left_ref[0])
    right = jnp.where(i == last, jnp.zeros_like(right_ref[0]), right_ref[0])
    n_ext = jnp.concatenate([left, n, right], axis=0)

    for c in range(N_FF_CHUNKS):
        gate_ref = gate1_ref if c % 2 else gate0_ref
        cols = slice(c * FF_CHUNK, (c + 1) * FF_CHUNK)
        gate_ref[...] = jnp.dot(n_ext, wg_ref[:, cols], preferred_element_type=F32)
        cw = cw_ref[:, cols]
        g = (gate_ref[HALO - 1:HALO - 1 + t, :] * cw[0:1]
             + gate_ref[HALO:HALO + t, :] * cw[1:2]
             + gate_ref[HALO + 1:HALO + 1 + t, :] * cw[2:3]
             + cw[3:4])
        up = jnp.dot(n, wu_ref[:, cols], preferred_element_type=F32)
        act = 0.5 * g * (1.0 + lax.erf(g * (2.0 ** -0.5)))
        u_ref[:, cols] = (act * up).astype(BF16)
    y = h_ref[0] + jnp.dot(u_ref[...], wd_ref[...], preferred_element_type=F32)
    ms = jnp.mean(y * y, axis=-1, keepdims=True)
    o_ref[0] = y * lax.rsqrt(ms + NORM_EPS) * gfin_ref[...]


def _ffn_call(h, n2, n2_meta, wg, wu, cw, wd, g_final, tile):
    b, s, d = h.shape
    grid = (b, s // tile)
    per = tile // HALO
    n_halo_blocks = s // HALO
    return pl.pallas_call(
        _ffn_kernel,
        grid=grid,
        in_specs=[
            pl.BlockSpec((1, tile, d), lambda bi, i: (bi, i, 0)),
            pl.BlockSpec((1, tile, d), lambda bi, i: (bi, i, 0)),
            pl.BlockSpec((1, HALO, d), lambda bi, i: (bi, jnp.maximum(i * per - 1, 0), 0)),
            pl.BlockSpec((1, HALO, d), lambda bi, i: (bi, 0, 0)),
            pl.BlockSpec((1, HALO, d),
                         lambda bi, i: (bi, jnp.minimum((i + 1) * per, n_halo_blocks - 1), 0)),
            _const_spec((d, D_FF)),
            _const_spec((d, D_FF)),
            _const_spec((8, D_FF)),
            _const_spec((D_FF, d)),
            _const_spec((1, d)),
        ],
        out_specs=pl.BlockSpec((1, tile, d), lambda bi, i: (bi, i, 0)),
        out_shape=jax.ShapeDtypeStruct((b, s, d), F32),
        scratch_shapes=[
            pltpu.VMEM((tile + 2 * HALO, FF_CHUNK), F32),
            pltpu.VMEM((tile + 2 * HALO, FF_CHUNK), F32),
            pltpu.VMEM((tile, D_FF), BF16),
        ],
        compiler_params=_params(2),
        name="ffn",
    )(h, n2, n2, n2_meta, n2, wg, wu, cw, wd, g_final)


def _pair_layout(ang):
    c = jnp.cos(ang)
    s = jnp.sin(ang)
    return jnp.tile(c, (1, 4)), jnp.concatenate([-s, -s, s, s], axis=-1)


def _linear_inv():
    return ROPE_THETA ** (-jnp.arange(0, HEAD_DIM, 2, dtype=F32) / HEAD_DIM)


def _real_tables(s):
    t = jnp.arange(s)
    rowp = (t // GRID_W).astype(F32)
    colp = (t % GRID_W).astype(F32)
    axis_dim = HEAD_DIM // 2
    inv_a = ROPE_THETA ** (-jnp.arange(0, axis_dim, 2, dtype=F32) / axis_dim)
    ang_a = jnp.concatenate([rowp[:, None] * inv_a[None], colp[:, None] * inv_a[None]], axis=-1)
    pos = jnp.arange(N_META + s, dtype=F32)[N_META:]
    ang_b = pos[:, None] * _linear_inv()[None]
    return _pair_layout(ang_a) + _pair_layout(ang_b)


def _meta_tables():
    ang_a = jnp.zeros((META_ROWS, HEAD_DIM // 2), F32)
    pos = jnp.arange(META_ROWS, dtype=F32)
    ang_b = pos[:, None] * _linear_inv()[None]
    return _pair_layout(ang_a) + _pair_layout(ang_b)


def _attn_step_plan(s, tq):
    kv_bytes = N_KV * s * LANES * 2
    per_tile = tq * D_MODEL * (2 * 2 * 4 + 2 * 2 * 2 + 2)

    def tiles(vt_buffers):
        fixed = 2 * s * 2 * tq * 4 + (2 + vt_buffers) * kv_bytes + ATTN_TEMP_BYTES
        n = 1
        while 2 * n * tq <= s and fixed + 2 * n * per_tile <= VMEM_LIMIT:
            n *= 2
        return n

    return (tiles(1), 1) if tiles(1) > tiles(2) else (tiles(2), 2)


def _trunk(x, meta_x, meta_qkv, prm, cfg):
    s = x.shape[1]
    q, k, vt = _proj_call(x, _real_tables(s), prm["w_in"], prm["g_mix"], prm["gq"], prm["gk"],
                          prm["bd"], min(s, cfg["proj_tile"]))
    q_m, k_m, vt_m = meta_qkv
    kc = min(s, cfg["key_chunk"])
    tq = min(s, cfg["tq"])
    n_tiles, vt_buffers = _attn_step_plan(s, tq)
    h1, n2 = _attn_call(x, q, k, vt, k_m, vt_m, prm["lam"], prm["g_subln"], prm["w_out"],
                        prm["g_ffn"], tq, n_tiles, vt_buffers, kc, shared_q=False)
    _, n2_m = _attn_call(meta_x[:, :META_Q], q_m[:, :, :META_Q], k, vt, k_m, vt_m, prm["lam"],
                         prm["g_subln"], prm["w_out"], prm["g_ffn"], META_Q, 1, 2, kc,
                         shared_q=True)
    return _ffn_call(h1, n2, n2_m, prm["wg"], prm["wu"], prm["cw"], prm["wd"], prm["g_final"],
                     min(s, cfg["row_tile"]))


_CFG = dict(proj_tile=1024, tq=256, key_chunk=512, row_tile=1024)


def kernel(x_prompt, x_sample, meta_tokens, g_mix, w_in, g_qnorm_a, g_knorm_a, lambda_q1, lambda_k1,
           lambda_q2, lambda_k2, g_subln, w_out, g_ffn, w_ff_gate, w_ff_up, conv_w, conv_b,
           w_ff_down, g_final):
    assert w_in.shape[0] == 1, "single-layer trunk"
    d = D_MODEL
    head_order = np.array([0, 4, 1, 5, 2, 6, 3, 7])
    a_cols = (head_order[:, None] * HEAD_DIM + np.arange(HEAD_DIM)[None]).reshape(-1)
    in_perm = np.concatenate([a_cols, np.arange(A_PAIRS * LANES, IN_COLS)])
    out_perm = np.concatenate([a_cols, np.arange(A_PAIRS * LANES, d)])
    half = HEAD_DIM // 2
    tile_perm = np.concatenate([np.arange(0, half), np.arange(2 * half, 3 * half),
                                np.arange(half, 2 * half), np.arange(3 * half, 4 * half)])
    qk_tiles = list(range(0, 5)) + list(range(6, 14))
    for tile in qk_tiles:
        in_perm[tile * LANES:(tile + 1) * LANES] = in_perm[tile * LANES + tile_perm]
    lane_dim = np.concatenate([np.arange(half), np.arange(half),
                               np.arange(half, 2 * half), np.arange(half, 2 * half)])
    blk = (np.arange(LANES) // half) % 2
    cw = jnp.concatenate([conv_w[0], conv_b[0][None], jnp.zeros((4, D_FF), F32)], axis=0)
    lam = jnp.stack([lambda_q1[0], lambda_k1[0], lambda_q2[0], lambda_k2[0]])
    prm = dict(
        w_in=w_in[0][:, in_perm].astype(BF16),
        g_mix=g_mix[0][None],
        gq=g_qnorm_a[0][lane_dim][None],
        gk=g_knorm_a[0][lane_dim][None],
        bd=jnp.asarray((blk[:, None] == blk[None]) / HEAD_DIM, BF16),
        lam=jnp.zeros((8, LANES), F32).at[:4, :HEAD_DIM].set(lam),
        g_subln=g_subln[0][None],
        w_out=w_out[0][out_perm].astype(BF16),
        g_ffn=g_ffn[0][None],
        wg=w_ff_gate[0].astype(BF16),
        wu=w_ff_up[0].astype(BF16),
        cw=cw,
        wd=w_ff_down[0].astype(BF16),
        g_final=g_final[None],
    )
    meta_x = jnp.zeros((1, META_ROWS, d), F32).at[0, :N_META].set(meta_tokens)
    meta_qkv = _proj_call(meta_x, _meta_tables(), prm["w_in"], prm["g_mix"], prm["gq"], prm["gk"],
                          prm["bd"], META_ROWS)
    y_prompt = _trunk(x_prompt, meta_x, meta_qkv, prm, _CFG)
    y_sample = _trunk(x_sample, meta_x, meta_qkv, prm, _CFG)
    return (y_prompt, y_sample)
```

```python
import functools
import math

import jax
import jax.numpy as jnp
import numpy as np
from jax import lax
from jax.experimental import pallas as pl
from jax.experimental.pallas import tpu as pltpu

F32 = jnp.float32
BF16 = jnp.bfloat16

D_MODEL = 1024
HEAD_DIM = 64
N_META = 16
GRID_W = 64
ROPE_THETA = 10000.0
NORM_EPS = 1e-6
LANES = 128
N_PAIRS = 8
N_KV = 5
V_ROWS = LANES + 16
A_PAIRS = 4
D_FF = 2816
FF_CHUNK = 256
N_FF_CHUNKS = D_FF // FF_CHUNK
IN_COLS = 2304
HALO = 16
META_ROWS = 128
META_Q = 64
LAMBDA_INIT = 0.8 - 0.6 * math.exp(-0.3 * 0)
Q_SCALE = (HEAD_DIM ** -0.5) * math.log2(math.e)
VMEM_LIMIT = 56 * 1024 * 1024
ATTN_TEMP_BYTES = 11 * 1024 * 1024


def _params(n_grid_axes):
    return pltpu.CompilerParams(
        dimension_semantics=("arbitrary",) * n_grid_axes,
        vmem_limit_bytes=VMEM_LIMIT,
    )


def _const_spec(shape):
    zeros = (0,) * len(shape)
    return pl.BlockSpec(shape, lambda *_: zeros, pipeline_mode=pl.Buffered(1))


def _proj_kernel(x_ref, gmix_ref, w_ref, gq_ref, gk_ref, bd_ref, ca_ref, sa_ref, cb_ref, sb_ref,
                 q_ref, k_ref, vt_ref):
    x = x_ref[0]
    t = x.shape[0]
    ms = jnp.mean(x * x, axis=-1, keepdims=True)
    n = (x * lax.rsqrt(ms + NORM_EPS) * gmix_ref[...]).astype(BF16)

    def rope(y, c, s):
        return y * c + pltpu.roll(y, LANES // 2, 1) * s

    def proj(col):
        y = jnp.dot(n, w_ref[:, col:col + 2 * LANES], preferred_element_type=F32)
        return y[:, :LANES], y[:, LANES:]

    ones_rows = (lax.broadcasted_iota(jnp.int32, (V_ROWS - LANES, t), 0) == 0).astype(BF16)

    def put_vt(kv, y):
        vt_ref[0, kv, :LANES, :] = y.T.astype(BF16)
        vt_ref[0, kv, LANES:, :] = ones_rows

    ca, sa, cb, sb = ca_ref[...], sa_ref[...], cb_ref[...], sb_ref[...]
    gq, gk = gq_ref[...], gk_ref[...]

    ya = list(proj(0) + proj(2 * LANES))
    yk, yv = proj(4 * LANES)
    put_vt(0, yv)
    sq = jnp.concatenate([(y * y).astype(BF16) for y in ya + [yk]], axis=0)
    ms_all = jnp.dot(sq, bd_ref[...], preferred_element_type=F32)
    for p in range(A_PAIRS):
        r = lax.rsqrt(ms_all[p * t:(p + 1) * t] + NORM_EPS) * Q_SCALE
        q_ref[0, p] = (rope(ya[p] * gq, ca, sa) * r).astype(BF16)
    r = lax.rsqrt(ms_all[A_PAIRS * t:] + NORM_EPS)
    k_ref[0, 0] = (rope(yk * gk, ca, sa) * r).astype(BF16)
    for j in range(2):
        y0, y1 = proj(14 * LANES + 2 * LANES * j)
        put_vt(1 + 2 * j, y0)
        put_vt(2 + 2 * j, y1)
    for j in range(2):
        y0, y1 = proj(10 * LANES + 2 * LANES * j)
        k_ref[0, 1 + 2 * j] = rope(y0, cb, sb).astype(BF16)
        k_ref[0, 2 + 2 * j] = rope(y1, cb, sb).astype(BF16)
    for j in range(2):
        y0, y1 = proj(6 * LANES + 2 * LANES * j)
        q_ref[0, A_PAIRS + 2 * j] = (rope(y0, cb, sb) * Q_SCALE).astype(BF16)
        q_ref[0, A_PAIRS + 2 * j + 1] = (rope(y1, cb, sb) * Q_SCALE).astype(BF16)


def _proj_call(x, tables, w_in, gmix, gq, gk, bd, tile):
    b, s, d = x.shape
    ca, sa, cb, sb = tables
    grid = (b, s // tile)
    tab_spec = pl.BlockSpec((tile, LANES), lambda bi, i: (i, 0))
    return pl.pallas_call(
        _proj_kernel,
        grid=grid,
        in_specs=[
            pl.BlockSpec((1, tile, d), lambda bi, i: (bi, i, 0)),
            _const_spec((1, d)),
            _const_spec((d, IN_COLS)),
            _const_spec((1, LANES)),
            _const_spec((1, LANES)),
            _const_spec((LANES, LANES)),
            tab_spec, tab_spec, tab_spec, tab_spec,
        ],
        out_specs=[
            pl.BlockSpec((1, N_PAIRS, tile, LANES), lambda bi, i: (bi, 0, i, 0)),
            pl.BlockSpec((1, N_KV, tile, LANES), lambda bi, i: (bi, 0, i, 0)),
            pl.BlockSpec((1, N_KV, V_ROWS, tile), lambda bi, i: (bi, 0, 0, i)),
        ],
        out_shape=[
            jax.ShapeDtypeStruct((b, N_PAIRS, s, LANES), BF16),
            jax.ShapeDtypeStruct((b, N_KV, s, LANES), BF16),
            jax.ShapeDtypeStruct((b, N_KV, V_ROWS, s), BF16),
        ],
        compiler_params=_params(2),
        name="proj",
    )(x, gmix, w_in, gq, gk, bd, ca, sa, cb, sb)


def _attn_kernel(x_ref, q_ref, k_ref, vt_ref, km_ref, vtm_ref, lam_ref, gsub_ref, wo_ref, gffn_ref,
                 h_ref, n_ref, sa_ref, sb_ref, acca_ref, accb_ref, mix_ref,
                 *, tq, key_chunk):
    n_tiles = q_ref.shape[2] // tq
    s_len = k_ref.shape[2]
    n_chunks = s_len // key_chunk
    lane = lax.broadcasted_iota(jnp.int32, (tq, LANES), 1)
    q_low = (lane & (HEAD_DIM // 2)) == 0
    low = lane < HEAD_DIM
    contract_lanes = (((1,), (1,)), ((), ()))

    lam_p = lam_ref[...]
    lam = (jnp.exp(jnp.sum(lam_p[0:1] * lam_p[1:2], axis=-1, keepdims=True))
           - jnp.exp(jnp.sum(lam_p[2:3] * lam_p[3:4], axis=-1, keepdims=True))
           + LAMBDA_INIT)

    def chunk(c):
        return slice(c * key_chunk, (c + 1) * key_chunk)

    def locate(is_b, i):
        if isinstance(i, int):
            t, j = divmod(i, A_PAIRS)
        else:
            t, j = lax.shift_right_logical(i, 2), lax.bitwise_and(i, A_PAIRS - 1)
        return t, (j + A_PAIRS if is_b else j), (j + 1 if is_b else 0)

    def stage(scores_job=None, softmax_job=None, finish_job=None):
        if scores_job is not None:
            b1, i1, buf1 = scores_job
            t1, p1, kv1 = locate(b1, i1)
            row0 = t1 * tq if isinstance(t1, int) else pl.multiple_of(t1 * tq, tq)
            qp = q_ref[0, p1, pl.ds(row0, tq), :].astype(F32)
            qq = jnp.concatenate([jnp.where(q_low, qp, 0.0), jnp.where(q_low, 0.0, qp)],
                                 axis=0).astype(BF16)
            sm1 = lax.dot_general(km_ref[0, kv1][:N_META], qq, contract_lanes,
                                  preferred_element_type=F32)
            m8 = jnp.maximum(sm1[:8], sm1[8:])
        if softmax_job is not None:
            b2, i2, buf2, (sm2, m2) = softmax_job
            _, _, kv2 = locate(b2, i2)
            pm = jnp.exp2(sm2 - m2)
            pm_pad = jnp.concatenate(
                [pm.astype(BF16), jnp.zeros((META_ROWS - N_META, 2 * tq), BF16)], axis=0)
            acc = jnp.dot(vtm_ref[0, kv2], pm_pad, preferred_element_type=F32)
        out = None
        if finish_job is not None:
            b3, i3 = finish_job
            t3, p3, _ = locate(b3, i3)
            acc_ref = accb_ref if b3 else acca_ref
            inv_l = 1.0 / acc_ref[LANES:LANES + 1, :]
            ot = (acc_ref[:LANES, :] * inv_l).T
            o_lo, o_hi = ot[:tq], ot[tq:]
            if b3:
                dt = o_lo - lam * o_hi
                msd = jnp.mean(dt * dt, axis=-1, keepdims=True)
                out = (dt * lax.rsqrt(msd + NORM_EPS) * gsub_ref[...]
                       * (1.0 - LAMBDA_INIT)).astype(BF16)
            else:
                out = jnp.where(low, o_lo, o_hi).astype(BF16)
            mix_ref[t3, p3] = out
        for c in range(n_chunks):
            if scores_job is not None:
                sc = lax.dot_general(k_ref[0, kv1, chunk(c), :], qq, contract_lanes,
                                     preferred_element_type=F32)
                buf1[chunk(c), :] = sc
                m8 = jnp.maximum(m8, jnp.max(sc.reshape(key_chunk // 8, 8, 2 * tq), axis=0))
            if softmax_job is not None:
                pc = jnp.exp2(buf2[chunk(c), :] - m2)
                acc = acc + jnp.dot(vt_ref[0, kv2, :, chunk(c)], pc.astype(BF16),
                                    preferred_element_type=F32)
        if softmax_job is not None:
            acc_ref = accb_ref if b2 else acca_ref
            acc_ref[...] = acc
        if scores_job is not None:
            return sm1, jnp.max(m8, axis=0, keepdims=True)
        return out

    grp_a, grp_b = False, True
    last = n_tiles * A_PAIRS - 1
    stats = stage(scores_job=(grp_a, 0, sa_ref))
    stats = stage(scores_job=(grp_b, 0, sb_ref), softmax_job=(grp_a, 0, sa_ref, stats))
    stats = stage(scores_job=(grp_a, 1, sa_ref), softmax_job=(grp_b, 0, sb_ref, stats),
                  finish_job=(grp_a, 0))

    def body(i, stats_a):
        stats_b = stage(scores_job=(grp_b, i, sb_ref), softmax_job=(grp_a, i, sa_ref, stats_a),
                        finish_job=(grp_b, i - 1))
        return stage(scores_job=(grp_a, i + 1, sa_ref), softmax_job=(grp_b, i, sb_ref, stats_b),
                     finish_job=(grp_a, i))

    stats = lax.fori_loop(1, last, body, stats)
    stats = stage(scores_job=(grp_b, last, sb_ref), softmax_job=(grp_a, last, sa_ref, stats),
                  finish_job=(grp_b, last - 1))
    stage(softmax_job=(grp_b, last, sb_ref, stats), finish_job=(grp_a, last))
    out_last = stage(finish_job=(grp_b, last))

    split = (N_PAIRS - 1) * LANES
    for t in range(n_tiles):
        rows = slice(t * tq, (t + 1) * tq)
        if t + 1 < n_tiles:
            mix = jnp.concatenate([mix_ref[t, p] for p in range(N_PAIRS)], axis=1)
            h = x_ref[0, rows, :] + jnp.dot(mix, wo_ref[...], preferred_element_type=F32)
        else:
            mix = jnp.concatenate([mix_ref[t, p] for p in range(N_PAIRS - 1)], axis=1)
            h = (x_ref[0, rows, :]
                 + jnp.dot(mix, wo_ref[:split, :], preferred_element_type=F32)
                 + jnp.dot(out_last, wo_ref[split:, :], preferred_element_type=F32))
        h_ref[0, rows, :] = h
        ms = jnp.mean(h * h, axis=-1, keepdims=True)
        n_ref[0, rows, :] = (h * lax.rsqrt(ms + NORM_EPS) * gffn_ref[...]).astype(BF16)


def _attn_call(x, q, k, vt, km, vtm, lam_p, gsub, w_out, g_ffn, tq, n_tiles, vt_buffers, key_chunk,
               shared_q):
    _, _, sq, _ = q.shape
    b, _, s, _ = k.shape
    d = x.shape[-1]
    rows = tq * n_tiles
    grid = (b, sq // rows)
    if shared_q:
        q_map = lambda bi, i: (0, 0, i, 0)
        x_map = lambda bi, i: (0, i, 0)
    else:
        q_map = lambda bi, i: (bi, 0, i, 0)
        x_map = lambda bi, i: (bi, i, 0)
    return pl.pallas_call(
        functools.partial(_attn_kernel, tq=tq, key_chunk=key_chunk),
        grid=grid,
        in_specs=[
            pl.BlockSpec((1, rows, d), x_map),
            pl.BlockSpec((1, N_PAIRS, rows, LANES), q_map),
            pl.BlockSpec((1, N_KV, s, LANES), lambda bi, i: (bi, 0, 0, 0)),
            pl.BlockSpec((1, N_KV, V_ROWS, s), lambda bi, i: (bi, 0, 0, 0),
                         pipeline_mode=pl.Buffered(vt_buffers)),
            _const_spec((1, N_KV, META_ROWS, LANES)),
            _const_spec((1, N_KV, V_ROWS, META_ROWS)),
            _const_spec((8, LANES)),
            _const_spec((1, LANES)),
            _const_spec((d, d)),
            _const_spec((1, d)),
        ],
        out_specs=[
            pl.BlockSpec((1, rows, d), lambda bi, i: (bi, i, 0)),
            pl.BlockSpec((1, rows, d), lambda bi, i: (bi, i, 0)),
        ],
        out_shape=[
            jax.ShapeDtypeStruct((b, sq, d), F32),
            jax.ShapeDtypeStruct((b, sq, d), BF16),
        ],
        scratch_shapes=[
            pltpu.VMEM((s, 2 * tq), F32),
            pltpu.VMEM((s, 2 * tq), F32),
            pltpu.VMEM((V_ROWS, 2 * tq), F32),
            pltpu.VMEM((V_ROWS, 2 * tq), F32),
            pltpu.VMEM((n_tiles, N_PAIRS, tq, LANES), BF16),
        ],
        compiler_params=_params(2),
        name="attn",
    )(x, q, k, vt, km, vtm, lam_p, gsub, w_out, g_ffn)


def _ffn_kernel(h_ref, n_ref, left_ref, meta_ref, right_ref, wg_ref, wu_ref, cw_ref, wd_ref,
                gfin_ref, o_ref, gate0_ref, gate1_ref, u_ref):
    i = pl.program_id(1)
    last = pl.num_programs(1) - 1
    t = n_ref.shape[1]
    n = n_ref[0]
    left = jnp.where(i == 0, meta_ref[0], left_ref[0])
    right = jnp.where(i == last, jnp.zeros_like(right_ref[0]), right_ref[0])
    n_ext = jnp.concatenate([left, n, right], axis=0)

    for c in range(N_FF_CHUNKS):
        gate_ref = gate1_ref if c % 2 else gate0_ref
        cols = slice(c * FF_CHUNK, (c + 1) * FF_CHUNK)
        gate_ref[...] = jnp.dot(n_ext, wg_ref[:, cols], preferred_element_type=F32)
        cw = cw_ref[:, cols]
        g = (gate_ref[HALO - 1:HALO - 1 + t, :] * cw[0:1]
             + gate_ref[HALO:HALO + t, :] * cw[1:2]
             + gate_ref[HALO + 1:HALO + 1 + t, :] * cw[2:3]
             + cw[3:4])
        up = jnp.dot(n, wu_ref[:, cols], preferred_element_type=F32)
        act = 0.5 * g * (1.0 + lax.erf(g * (2.0 ** -0.5)))
        u_ref[:, cols] = (act * up).astype(BF16)
    y = h_ref[0] + jnp.dot(u_ref[...], wd_ref[...], preferred_element_type=F32)
    ms = jnp.mean(y * y, axis=-1, keepdims=True)
    o_ref[0] = y * lax.rsqrt(ms + NORM_EPS) * gfin_ref[...]


def _ffn_call(h, n2, n2_meta, wg, wu, cw, wd, g_final, tile):
    b, s, d = h.shape
    grid = (b, s // tile)
    per = tile // HALO
    n_halo_blocks = s // HALO
    return pl.pallas_call(
        _ffn_kernel,
        grid=grid,
        in_specs=[
            pl.BlockSpec((1, tile, d), lambda bi, i: (bi, i, 0)),
            pl.BlockSpec((1, tile, d), lambda bi, i: (bi, i, 0)),
            pl.BlockSpec((1, HALO, d), lambda bi, i: (bi, jnp.maximum(i * per - 1, 0), 0)),
            pl.BlockSpec((1, HALO, d), lambda bi, i: (bi, 0, 0)),
            pl.BlockSpec((1, HALO, d),
                         lambda bi, i: (bi, jnp.minimum((i + 1) * per, n_halo_blocks - 1), 0)),
            _const_spec((d, D_FF)),
            _const_spec((d, D_FF)),
            _const_spec((8, D_FF)),
            _const_spec((D_FF, d)),
            _const_spec((1, d)),
        ],
        out_specs=pl.BlockSpec((1, tile, d), lambda bi, i: (bi, i, 0)),
        out_shape=jax.ShapeDtypeStruct((b, s, d), F32),
        scratch_shapes=[
            pltpu.VMEM((tile + 2 * HALO, FF_CHUNK), F32),
            pltpu.VMEM((tile + 2 * HALO, FF_CHUNK), F32),
            pltpu.VMEM((tile, D_FF), BF16),
        ],
        compiler_params=_params(2),
        name="ffn",
    )(h, n2, n2, n2_meta, n2, wg, wu, cw, wd, g_final)


def _pair_layout(ang):
    c = jnp.cos(ang)
    s = jnp.sin(ang)
    return jnp.tile(c, (1, 4)), jnp.concatenate([-s, -s, s, s], axis=-1)


def _linear_inv():
    return ROPE_THETA ** (-jnp.arange(0, HEAD_DIM, 2, dtype=F32) / HEAD_DIM)


def _real_tables(s):
    t = jnp.arange(s)
    rowp = (t // GRID_W).astype(F32)
    colp = (t % GRID_W).astype(F32)
    axis_dim = HEAD_DIM // 2
    inv_a = ROPE_THETA ** (-jnp.arange(0, axis_dim, 2, dtype=F32) / axis_dim)
    ang_a = jnp.concatenate([rowp[:, None] * inv_a[None], colp[:, None] * inv_a[None]], axis=-1)
    pos = jnp.arange(N_META + s, dtype=F32)[N_META:]
    ang_b = pos[:, None] * _linear_inv()[None]
    return _pair_layout(ang_a) + _pair_layout(ang_b)


def _meta_tables():
    ang_a = jnp.zeros((META_ROWS, HEAD_DIM // 2), F32)
    pos = jnp.arange(META_ROWS, dtype=F32)
    ang_b = pos[:, None] * _linear_inv()[None]
    return _pair_layout(ang_a) + _pair_layout(ang_b)


def _attn_step_plan(s, tq):
    k_bytes = N_KV * s * LANES * 2
    vt_bytes = N_KV * s * V_ROWS * 2
    per_tile = tq * D_MODEL * (2 * 2 * 4 + 2 * 2 * 2 + 2)

    def tiles(vt_buffers):
        fixed = 2 * s * 2 * tq * 4 + 2 * k_bytes + vt_buffers * vt_bytes + ATTN_TEMP_BYTES
        n = 1
        while 2 * n * tq <= s and fixed + 2 * n * per_tile <= VMEM_LIMIT:
            n *= 2
        return n

    return (tiles(1), 1) if tiles(1) > tiles(2) else (tiles(2), 2)


def _trunk(x, meta_x, meta_qkv, prm, cfg):
    s = x.shape[1]
    q, k, vt = _proj_call(x, _real_tables(s), prm["w_in"], prm["g_mix"], prm["gq"], prm["gk"],
                          prm["bd"], min(s, cfg["proj_tile"]))
    q_m, k_m, vt_m = meta_qkv
    kc = min(s, cfg["key_chunk"])
    tq = min(s, cfg["tq"])
    n_tiles, vt_buffers = _attn_step_plan(s, tq)
    h1, n2 = _attn_call(x, q, k, vt, k_m, vt_m, prm["lam"], prm["g_subln"], prm["w_out"],
                        prm["g_ffn"], tq, n_tiles, vt_buffers, kc, shared_q=False)
    _, n2_m = _attn_call(meta_x[:, :META_Q], q_m[:, :, :META_Q], k, vt, k_m, vt_m, prm["lam"],
                         prm["g_subln"], prm["w_out"], prm["g_ffn"], META_Q, 1, 2, kc,
                         shared_q=True)
    return _ffn_call(h1, n2, n2_m, prm["wg"], prm["wu"], prm["cw"], prm["wd"], prm["g_final"],
                     min(s, cfg["row_tile"]))


_CFG = dict(proj_tile=1024, tq=256, key_chunk=512, row_tile=1024)


def kernel(x_prompt, x_sample, meta_tokens, g_mix, w_in, g_qnorm_a, g_knorm_a, lambda_q1, lambda_k1,
           lambda_q2, lambda_k2, g_subln, w_out, g_ffn, w_ff_gate, w_ff_up, conv_w, conv_b,
           w_ff_down, g_final):
    assert w_in.shape[0] == 1, "single-layer trunk"
    d = D_MODEL
    head_order = np.array([0, 4, 1, 5, 2, 6, 3, 7])
    a_cols = (head_order[:, None] * HEAD_DIM + np.arange(HEAD_DIM)[None]).reshape(-1)
    in_perm = np.concatenate([a_cols, np.arange(A_PAIRS * LANES, IN_COLS)])
    out_perm = np.concatenate([a_cols, np.arange(A_PAIRS * LANES, d)])
    half = HEAD_DIM // 2
    tile_perm = np.concatenate([np.arange(0, half), np.arange(2 * half, 3 * half),
                                np.arange(half, 2 * half), np.arange(3 * half, 4 * half)])
    qk_tiles = list(range(0, 5)) + list(range(6, 14))
    for tile in qk_tiles:
        in_perm[tile * LANES:(tile + 1) * LANES] = in_perm[tile * LANES + tile_perm]
    lane_dim = np.concatenate([np.arange(half), np.arange(half),
                               np.arange(half, 2 * half), np.arange(half, 2 * half)])
    blk = (np.arange(LANES) // half) % 2
    cw = jnp.concatenate([conv_w[0], conv_b[0][None], jnp.zeros((4, D_FF), F32)], axis=0)
    lam = jnp.stack([lambda_q1[0], lambda_k1[0], lambda_q2[0], lambda_k2[0]])
    prm = dict(
        w_in=w_in[0][:, in_perm].astype(BF16),
        g_mix=g_mix[0][None],
        gq=g_qnorm_a[0][lane_dim][None],
        gk=g_knorm_a[0][lane_dim][None],
        bd=jnp.asarray((blk[:, None] == blk[None]) / HEAD_DIM, BF16),
        lam=jnp.zeros((8, LANES), F32).at[:4, :HEAD_DIM].set(lam),
        g_subln=g_subln[0][None],
        w_out=w_out[0][out_perm].astype(BF16),
        g_ffn=g_ffn[0][None],
        wg=w_ff_gate[0].astype(BF16),
        wu=w_ff_up[0].astype(BF16),
        cw=cw,
        wd=w_ff_down[0].astype(BF16),
        g_final=g_final[None],
    )
    meta_x = jnp.zeros((1, META_ROWS, d), F32).at[0, :N_META].set(meta_tokens)
    meta_qkv = _proj_call(meta_x, _meta_tables(), prm["w_in"], prm["g_mix"], prm["gq"], prm["gk"],
                          prm["bd"], META_ROWS)
    y_prompt = _trunk(x_prompt, meta_x, meta_qkv, prm, _CFG)
    y_sample = _trunk(x_sample, meta_x, meta_qkv, prm, _CFG)
    return (y_prompt, y_sample)
```

```python
import functools
import math

import jax
import jax.numpy as jnp
import numpy as np
from jax import lax
from jax.experimental import pallas as pl
from jax.experimental.pallas import tpu as pltpu

F32 = jnp.float32
BF16 = jnp.bfloat16

D_MODEL = 1024
HEAD_DIM = 64
N_META = 16
GRID_W = 64
ROPE_THETA = 10000.0
NORM_EPS = 1e-6
LANES = 128
N_PAIRS = 8
N_KV = 5
A_PAIRS = 4
D_FF = 2816
FF_CHUNK = 256
N_FF_CHUNKS = D_FF // FF_CHUNK
IN_COLS = 2304
HALO = 16
META_ROWS = 128
META_Q = 64
LAMBDA_INIT = 0.8 - 0.6 * math.exp(-0.3 * 0)
Q_SCALE = (HEAD_DIM ** -0.5) * math.log2(math.e)
VMEM_LIMIT = 56 * 1024 * 1024
ATTN_TEMP_BYTES = 11 * 1024 * 1024


def _params(n_grid_axes):
    return pltpu.CompilerParams(
        dimension_semantics=("arbitrary",) * n_grid_axes,
        vmem_limit_bytes=VMEM_LIMIT,
    )


def _const_spec(shape):
    zeros = (0,) * len(shape)
    return pl.BlockSpec(shape, lambda *_: zeros, pipeline_mode=pl.Buffered(1))


def _proj_kernel(x_ref, gmix_ref, w_ref, gq_ref, gk_ref, bd_ref, ca_ref, sa_ref, cb_ref, sb_ref,
                 q_ref, k_ref, vt_ref):
    x = x_ref[0]
    t = x.shape[0]
    ms = jnp.mean(x * x, axis=-1, keepdims=True)
    n = (x * lax.rsqrt(ms + NORM_EPS) * gmix_ref[...]).astype(BF16)

    def rope(y, c, s):
        return y * c + pltpu.roll(y, LANES // 2, 1) * s

    def proj(col):
        y = jnp.dot(n, w_ref[:, col:col + 2 * LANES], preferred_element_type=F32)
        return y[:, :LANES], y[:, LANES:]

    ca, sa, cb, sb = ca_ref[...], sa_ref[...], cb_ref[...], sb_ref[...]
    gq, gk = gq_ref[...], gk_ref[...]

    ya = list(proj(0) + proj(2 * LANES))
    yk, yv = proj(4 * LANES)
    vt_ref[0, 0] = yv.T.astype(BF16)
    sq = jnp.concatenate([(y * y).astype(BF16) for y in ya + [yk]], axis=0)
    ms_all = jnp.dot(sq, bd_ref[...], preferred_element_type=F32)
    for p in range(A_PAIRS):
        r = lax.rsqrt(ms_all[p * t:(p + 1) * t] + NORM_EPS) * Q_SCALE
        q_ref[0, p] = (rope(ya[p] * gq, ca, sa) * r).astype(BF16)
    r = lax.rsqrt(ms_all[A_PAIRS * t:] + NORM_EPS)
    k_ref[0, 0] = (rope(yk * gk, ca, sa) * r).astype(BF16)
    for j in range(2):
        y0, y1 = proj(14 * LANES + 2 * LANES * j)
        vt_ref[0, 1 + 2 * j] = y0.T.astype(BF16)
        vt_ref[0, 2 + 2 * j] = y1.T.astype(BF16)
    for j in range(2):
        y0, y1 = proj(10 * LANES + 2 * LANES * j)
        k_ref[0, 1 + 2 * j] = rope(y0, cb, sb).astype(BF16)
        k_ref[0, 2 + 2 * j] = rope(y1, cb, sb).astype(BF16)
    for j in range(2):
        y0, y1 = proj(6 * LANES + 2 * LANES * j)
        q_ref[0, A_PAIRS + 2 * j] = (rope(y0, cb, sb) * Q_SCALE).astype(BF16)
        q_ref[0, A_PAIRS + 2 * j + 1] = (rope(y1, cb, sb) * Q_SCALE).astype(BF16)


def _proj_call(x, tables, w_in, gmix, gq, gk, bd, tile):
    b, s, d = x.shape
    ca, sa, cb, sb = tables
    grid = (b, s // tile)
    tab_spec = pl.BlockSpec((tile, LANES), lambda bi, i: (i, 0))
    return pl.pallas_call(
        _proj_kernel,
        grid=grid,
        in_specs=[
            pl.BlockSpec((1, tile, d), lambda bi, i: (bi, i, 0)),
            _const_spec((1, d)),
            _const_spec((d, IN_COLS)),
            _const_spec((1, LANES)),
            _const_spec((1, LANES)),
            _const_spec((LANES, LANES)),
            tab_spec, tab_spec, tab_spec, tab_spec,
        ],
        out_specs=[
            pl.BlockSpec((1, N_PAIRS, tile, LANES), lambda bi, i: (bi, 0, i, 0)),
            pl.BlockSpec((1, N_KV, tile, LANES), lambda bi, i: (bi, 0, i, 0)),
            pl.BlockSpec((1, N_KV, LANES, tile), lambda bi, i: (bi, 0, 0, i)),
        ],
        out_shape=[
            jax.ShapeDtypeStruct((b, N_PAIRS, s, LANES), BF16),
            jax.ShapeDtypeStruct((b, N_KV, s, LANES), BF16),
            jax.ShapeDtypeStruct((b, N_KV, LANES, s), BF16),
        ],
        compiler_params=_params(2),
        name="proj",
    )(x, gmix, w_in, gq, gk, bd, ca, sa, cb, sb)


def _attn_kernel(x_ref, q_ref, k_ref, vt_ref, km_ref, vtm_ref, lam_ref, gsub_ref, wo_ref, gffn_ref,
                 h_ref, n_ref, sa_ref, sb_ref, acca_ref, accb_ref, la_ref, lb_ref, mix_ref,
                 *, tq, key_chunk):
    n_tiles = q_ref.shape[2] // tq
    s_len = k_ref.shape[2]
    n_chunks = s_len // key_chunk
    lane = lax.broadcasted_iota(jnp.int32, (tq, LANES), 1)
    q_low = (lane & (HEAD_DIM // 2)) == 0
    low = lane < HEAD_DIM

    lam_p = lam_ref[...]
    lam = (jnp.exp(jnp.sum(lam_p[0:1] * lam_p[1:2], axis=-1, keepdims=True))
           - jnp.exp(jnp.sum(lam_p[2:3] * lam_p[3:4], axis=-1, keepdims=True))
           + LAMBDA_INIT)

    def chunk(c):
        return slice(c * key_chunk, (c + 1) * key_chunk)

    def locate(is_b, i):
        if isinstance(i, int):
            t, j = divmod(i, A_PAIRS)
        else:
            t, j = lax.shift_right_logical(i, 2), lax.bitwise_and(i, A_PAIRS - 1)
        return t, (j + A_PAIRS if is_b else j), (j + 1 if is_b else 0)

    def stage(scores_job=None, softmax_job=None, finish_job=None):
        if scores_job is not None:
            b1, i1, buf1 = scores_job
            t1, p1, kv1 = locate(b1, i1)
            row0 = t1 * tq if isinstance(t1, int) else pl.multiple_of(t1 * tq, tq)
            qp = q_ref[0, p1, pl.ds(row0, tq), :].astype(F32)
            qq = jnp.concatenate([jnp.where(q_low, qp, 0.0), jnp.where(q_low, 0.0, qp)],
                                 axis=0).T.astype(BF16)
            sm1 = jnp.dot(km_ref[0, kv1][:N_META], qq, preferred_element_type=F32)
            m8 = jnp.maximum(sm1[:8], sm1[8:])
        if softmax_job is not None:
            b2, i2, buf2, (sm2, m2) = softmax_job
            _, _, kv2 = locate(b2, i2)
            pm = jnp.exp2(sm2 - m2)
            l8 = pm[:8] + pm[8:]
            pm_pad = jnp.concatenate(
                [pm.astype(BF16), jnp.zeros((META_ROWS - N_META, 2 * tq), BF16)], axis=0)
            acc = jnp.dot(vtm_ref[0, kv2], pm_pad, preferred_element_type=F32)
        out = None
        if finish_job is not None:
            b3, i3 = finish_job
            t3, p3, _ = locate(b3, i3)
            acc_ref, l_ref = (accb_ref, lb_ref) if b3 else (acca_ref, la_ref)
            inv_l = 1.0 / jnp.sum(l_ref[...], axis=0, keepdims=True)
            ot = (acc_ref[...] * inv_l).T
            o_lo, o_hi = ot[:tq], ot[tq:]
            if b3:
                dt = o_lo - lam * o_hi
                msd = jnp.mean(dt * dt, axis=-1, keepdims=True)
                out = (dt * lax.rsqrt(msd + NORM_EPS) * gsub_ref[...]
                       * (1.0 - LAMBDA_INIT)).astype(BF16)
            else:
                out = jnp.where(low, o_lo, o_hi).astype(BF16)
            mix_ref[t3, p3] = out
        for c in range(n_chunks):
            if scores_job is not None:
                sc = jnp.dot(k_ref[0, kv1, chunk(c), :], qq,
                             preferred_element_type=F32)
                buf1[chunk(c), :] = sc
                m8 = jnp.maximum(m8, jnp.max(sc.reshape(key_chunk // 8, 8, 2 * tq), axis=0))
            if softmax_job is not None:
                pc = jnp.exp2(buf2[chunk(c), :] - m2)
                l8 = l8 + jnp.sum(pc.reshape(key_chunk // 8, 8, 2 * tq), axis=0)
                acc = acc + jnp.dot(vt_ref[0, kv2, :, chunk(c)], pc.astype(BF16),
                                    preferred_element_type=F32)
        if softmax_job is not None:
            acc_ref, l_ref = (accb_ref, lb_ref) if b2 else (acca_ref, la_ref)
            acc_ref[...] = acc
            l_ref[...] = l8
        if scores_job is not None:
            return sm1, jnp.max(m8, axis=0, keepdims=True)
        return out

    grp_a, grp_b = False, True
    last = n_tiles * A_PAIRS - 1
    stats = stage(scores_job=(grp_a, 0, sa_ref))
    stats = stage(scores_job=(grp_b, 0, sb_ref), softmax_job=(grp_a, 0, sa_ref, stats))
    stats = stage(scores_job=(grp_a, 1, sa_ref), softmax_job=(grp_b, 0, sb_ref, stats),
                  finish_job=(grp_a, 0))

    def body(i, stats_a):
        stats_b = stage(scores_job=(grp_b, i, sb_ref), softmax_job=(grp_a, i, sa_ref, stats_a),
                        finish_job=(grp_b, i - 1))
        return stage(scores_job=(grp_a, i + 1, sa_ref), softmax_job=(grp_b, i, sb_ref, stats_b),
                     finish_job=(grp_a, i))

    stats = lax.fori_loop(1, last, body, stats)
    stats = stage(scores_job=(grp_b, last, sb_ref), softmax_job=(grp_a, last, sa_ref, stats),
                  finish_job=(grp_b, last - 1))
    stage(softmax_job=(grp_b, last, sb_ref, stats), finish_job=(grp_a, last))
    out_last = stage(finish_job=(grp_b, last))

    split = (N_PAIRS - 1) * LANES
    for t in range(n_tiles):
        rows = slice(t * tq, (t + 1) * tq)
        if t + 1 < n_tiles:
            mix = jnp.concatenate([mix_ref[t, p] for p in range(N_PAIRS)], axis=1)
            h = x_ref[0, rows, :] + jnp.dot(mix, wo_ref[...], preferred_element_type=F32)
        else:
            mix = jnp.concatenate([mix_ref[t, p] for p in range(N_PAIRS - 1)], axis=1)
            h = (x_ref[0, rows, :]
                 + jnp.dot(mix, wo_ref[:split, :], preferred_element_type=F32)
                 + jnp.dot(out_last, wo_ref[split:, :], preferred_element_type=F32))
        h_ref[0, rows, :] = h
        ms = jnp.mean(h * h, axis=-1, keepdims=True)
        n_ref[0, rows, :] = (h * lax.rsqrt(ms + NORM_EPS) * gffn_ref[...]).astype(BF16)


def _attn_call(x, q, k, vt, km, vtm, lam_p, gsub, w_out, g_ffn, tq, n_tiles, vt_buffers, key_chunk,
               shared_q):
    _, _, sq, _ = q.shape
    b, _, s, _ = k.shape
    d = x.shape[-1]
    rows = tq * n_tiles
    grid = (b, sq // rows)
    if shared_q:
        q_map = lambda bi, i: (0, 0, i, 0)
        x_map = lambda bi, i: (0, i, 0)
    else:
        q_map = lambda bi, i: (bi, 0, i, 0)
        x_map = lambda bi, i: (bi, i, 0)
    return pl.pallas_call(
        functools.partial(_attn_kernel, tq=tq, key_chunk=key_chunk),
        grid=grid,
        in_specs=[
            pl.BlockSpec((1, rows, d), x_map),
            pl.BlockSpec((1, N_PAIRS, rows, LANES), q_map),
            pl.BlockSpec((1, N_KV, s, LANES), lambda bi, i: (bi, 0, 0, 0)),
            pl.BlockSpec((1, N_KV, LANES, s), lambda bi, i: (bi, 0, 0, 0),
                         pipeline_mode=pl.Buffered(vt_buffers)),
            _const_spec((1, N_KV, META_ROWS, LANES)),
            _const_spec((1, N_KV, LANES, META_ROWS)),
            _const_spec((8, LANES)),
            _const_spec((1, LANES)),
            _const_spec((d, d)),
            _const_spec((1, d)),
        ],
        out_specs=[
            pl.BlockSpec((1, rows, d), lambda bi, i: (bi, i, 0)),
            pl.BlockSpec((1, rows, d), lambda bi, i: (bi, i, 0)),
        ],
        out_shape=[
            jax.ShapeDtypeStruct((b, sq, d), F32),
            jax.ShapeDtypeStruct((b, sq, d), BF16),
        ],
        scratch_shapes=[
            pltpu.VMEM((s, 2 * tq), F32),
            pltpu.VMEM((s, 2 * tq), F32),
            pltpu.VMEM((LANES, 2 * tq), F32),
            pltpu.VMEM((LANES, 2 * tq), F32),
            pltpu.VMEM((8, 2 * tq), F32),
            pltpu.VMEM((8, 2 * tq), F32),
            pltpu.VMEM((n_tiles, N_PAIRS, tq, LANES), BF16),
        ],
        compiler_params=_params(2),
        name="attn",
    )(x, q, k, vt, km, vtm, lam_p, gsub, w_out, g_ffn)


def _ffn_kernel(h_ref, n_ref, left_ref, meta_ref, right_ref, wg_ref, wu_ref, cw_ref, wd_ref,
                gfin_ref, o_ref, gate0_ref, gate1_ref, u_ref):
    i = pl.program_id(1)
    last = pl.num_programs(1) - 1
    t = n_ref.shape[1]
    n = n_ref[0]
    left = jnp.where(i == 0, meta_ref[0], left_ref[0])
    right = jnp.where(i == last, jnp.zeros_like(right_ref[0]), right_ref[0])
    n_ext = jnp.concatenate([left, n, right], axis=0)

    for c in range(N_FF_CHUNKS):
        gate_ref = gate1_ref if c % 2 else gate0_ref
        cols = slice(c * FF_CHUNK, (c + 1) * FF_CHUNK)
        gate_ref[...] = jnp.dot(n_ext, wg_ref[:, cols], preferred_element_type=F32)
        cw = cw_ref[:, cols]
        g = (gate_ref[HALO - 1:HALO - 1 + t, :] * cw[0:1]
             + gate_ref[HALO:HALO + t, :] * cw[1:2]
             + gate_ref[HALO + 1:HALO + 1 + t, :] * cw[2:3]
             + cw[3:4])
        up = jnp.dot(n, wu_ref[:, cols], preferred_element_type=F32)
        act = 0.5 * g * (1.0 + lax.erf(g * (2.0 ** -0.5)))
        u_ref[:, cols] = (act * up).astype(BF16)
    y = h_ref[0] + jnp.dot(u_ref[...], wd_ref[...], preferred_element_type=F32)
    ms = jnp.mean(y * y, axis=-1, keepdims=True)
    o_ref[0] = y * lax.rsqrt(ms + NORM_EPS) * gfin_ref[...]


def _ffn_call(h, n2, n2_meta, wg, wu, cw, wd, g_final, tile):
    b, s, d = h.shape
    grid = (b, s // tile)
    per = tile // HALO
    n_halo_blocks = s // HALO
    return pl.pallas_call(
        _ffn_kernel,
        grid=grid,
        in_specs=[
            pl.BlockSpec((1, tile, d), lambda bi, i: (bi, i, 0)),
            pl.BlockSpec((1, tile, d), lambda bi, i: (bi, i, 0)),
            pl.BlockSpec((1, HALO, d), lambda bi, i: (bi, jnp.maximum(i * per - 1, 0), 0)),
            pl.BlockSpec((1, HALO, d), lambda bi, i: (bi, 0, 0)),
            pl.BlockSpec((1, HALO, d),
                         lambda bi, i: (bi, jnp.minimum((i + 1) * per, n_halo_blocks - 1), 0)),
            _const_spec((d, D_FF)),
            _const_spec((d, D_FF)),
            _const_spec((8, D_FF)),
            _const_spec((D_FF, d)),
            _const_spec((1, d)),
        ],
        out_specs=pl.BlockSpec((1, tile, d), lambda bi, i: (bi, i, 0)),
        out_shape=jax.ShapeDtypeStruct((b, s, d), F32),
        scratch_shapes=[
            pltpu.VMEM((tile + 2 * HALO, FF_CHUNK), F32),
            pltpu.VMEM((tile + 2 * HALO, FF_CHUNK), F32),
            pltpu.VMEM((tile, D_FF), BF16),
        ],
        compiler_params=_params(2),
        name="ffn",
    )(h, n2, n2, n2_meta, n2, wg, wu, cw, wd, g_final)


def _pair_layout(ang):
    c = jnp.cos(ang)
    s = jnp.sin(ang)
    return jnp.tile(c, (1, 4)), jnp.concatenate([-s, -s, s, s], axis=-1)


def _linear_inv():
    return ROPE_THETA ** (-jnp.arange(0, HEAD_DIM, 2, dtype=F32) / HEAD_DIM)


def _real_tables(s):
    t = jnp.arange(s)
    rowp = (t // GRID_W).astype(F32)
    colp = (t % GRID_W).astype(F32)
    axis_dim = HEAD_DIM // 2
    inv_a = ROPE_THETA ** (-jnp.arange(0, axis_dim, 2, dtype=F32) / axis_dim)
    ang_a = jnp.concatenate([rowp[:, None] * inv_a[None], colp[:, None] * inv_a[None]], axis=-1)
    pos = jnp.arange(N_META + s, dtype=F32)[N_META:]
    ang_b = pos[:, None] * _linear_inv()[None]
    return _pair_layout(ang_a) + _pair_layout(ang_b)


def _meta_tables():
    ang_a = jnp.zeros((META_ROWS, HEAD_DIM // 2), F32)
    pos = jnp.arange(META_ROWS, dtype=F32)
    ang_b = pos[:, None] * _linear_inv()[None]
    return _pair_layout(ang_a) + _pair_layout(ang_b)


def _attn_step_plan(s, tq):
    kv_bytes = N_KV * s * LANES * 2
    per_tile = tq * D_MODEL * (2 * 2 * 4 + 2 * 2 * 2 + 2)

    def tiles(vt_buffers):
        fixed = 2 * s * 2 * tq * 4 + (2 + vt_buffers) * kv_bytes + ATTN_TEMP_BYTES
        n = 1
        while 2 * n * tq <= s and fixed + 2 * n * per_tile <= VMEM_LIMIT:
            n *= 2
        return n

    return (tiles(1), 1) if tiles(1) > tiles(2) else (tiles(2), 2)


def _trunk(x, meta_x, meta_qkv, prm, cfg):
    s = x.shape[1]
    q, k, vt = _proj_call(x, _real_tables(s), prm["w_in"], prm["g_mix"], prm["gq"], prm["gk"],
                          prm["bd"], min(s, cfg["proj_tile"]))
    q_m, k_m, vt_m = meta_qkv
    kc = min(s, cfg["key_chunk"])
    tq = min(s, cfg["tq"])
    n_tiles, vt_buffers = _attn_step_plan(s, tq)
    h1, n2 = _attn_call(x, q, k, vt, k_m, vt_m, prm["lam"], prm["g_subln"], prm["w_out"],
                        prm["g_ffn"], tq, n_tiles, vt_buffers, kc, shared_q=False)
    _, n2_m = _attn_call(meta_x[:, :META_Q], q_m[:, :, :META_Q], k, vt, k_m, vt_m, prm["lam"],
                         prm["g_subln"], prm["w_out"], prm["g_ffn"], META_Q, 1, 2, kc,
                         shared_q=True)
    return _ffn_call(h1, n2, n2_m, prm["wg"], prm["wu"], prm["cw"], prm["wd"], prm["g_final"],
                     min(s, cfg["row_tile"]))


_CFG = dict(proj_tile=1024, tq=256, key_chunk=512, row_tile=1024)


def kernel(x_prompt, x_sample, meta_tokens, g_mix, w_in, g_qnorm_a, g_knorm_a, lambda_q1, lambda_k1,
           lambda_q2, lambda_k2, g_subln, w_out, g_ffn, w_ff_gate, w_ff_up, conv_w, conv_b,
           w_ff_down, g_final):
    assert w_in.shape[0] == 1, "single-layer trunk"
    d = D_MODEL
    head_order = np.array([0, 4, 1, 5, 2, 6, 3, 7])
    a_cols = (head_order[:, None] * HEAD_DIM + np.arange(HEAD_DIM)[None]).reshape(-1)
    in_perm = np.concatenate([a_cols, np.arange(A_PAIRS * LANES, IN_COLS)])
    out_perm = np.concatenate([a_cols, np.arange(A_PAIRS * LANES, d)])
    half = HEAD_DIM // 2
    tile_perm = np.concatenate([np.arange(0, half), np.arange(2 * half, 3 * half),
                                np.arange(half, 2 * half), np.arange(3 * half, 4 * half)])
    qk_tiles = list(range(0, 5)) + list(range(6, 14))
    for tile in qk_tiles:
        in_perm[tile * LANES:(tile + 1) * LANES] = in_perm[tile * LANES + tile_perm]
    lane_dim = np.concatenate([np.arange(half), np.arange(half),
                               np.arange(half, 2 * half), np.arange(half, 2 * half)])
    blk = (np.arange(LANES) // half) % 2
    cw = jnp.concatenate([conv_w[0], conv_b[0][None], jnp.zeros((4, D_FF), F32)], axis=0)
    lam = jnp.stack([lambda_q1[0], lambda_k1[0], lambda_q2[0], lambda_k2[0]])
    prm = dict(
        w_in=w_in[0][:, in_perm].astype(BF16),
        g_mix=g_mix[0][None],
        gq=g_qnorm_a[0][lane_dim][None],
        gk=g_knorm_a[0][lane_dim][None],
        bd=jnp.asarray((blk[:, None] == blk[None]) / HEAD_DIM, BF16),
        lam=jnp.zeros((8, LANES), F32).at[:4, :HEAD_DIM].set(lam),
        g_subln=g_subln[0][None],
        w_out=w_out[0][out_perm].astype(BF16),
        g_ffn=g_ffn[0][None],
        wg=w_ff_gate[0].astype(BF16),
        wu=w_ff_up[0].astype(BF16),
        cw=cw,
        wd=w_ff_down[0].astype(BF16),
        g_final=g_final[None],
    )
    meta_x = jnp.zeros((1, META_ROWS, d), F32).at[0, :N_META].set(meta_tokens)
    meta_qkv = _proj_call(meta_x, _meta_tables(), prm["w_in"], prm["g_mix"], prm["gq"], prm["gk"],
                          prm["bd"], META_ROWS)
    y_prompt = _trunk(x_prompt, meta_x, meta_qkv, prm, _CFG)
    y_sample = _trunk(x_sample, meta_x, meta_qkv, prm, _CFG)
    return (y_prompt, y_sample)
```

```python
import functools
import math

import jax
import jax.numpy as jnp
import numpy as np
from jax import lax
from jax.experimental import pallas as pl
from jax.experimental.pallas import tpu as pltpu

F32 = jnp.float32
BF16 = jnp.bfloat16

D_MODEL = 1024
HEAD_DIM = 64
N_META = 16
GRID_W = 64
ROPE_THETA = 10000.0
NORM_EPS = 1e-6
LANES = 128
N_PAIRS = 8
N_KV = 5
A_PAIRS = 4
D_FF = 2816
FF_CHUNK = 256
N_FF_CHUNKS = D_FF // FF_CHUNK
IN_COLS = 2304
HALO = 16
META_ROWS = 128
META_Q = 64
LAMBDA_INIT = 0.8 - 0.6 * math.exp(-0.3 * 0)
Q_SCALE = (HEAD_DIM ** -0.5) * math.log2(math.e)
VMEM_LIMIT = 56 * 1024 * 1024
ATTN_TEMP_BYTES = 11 * 1024 * 1024


def _params(n_grid_axes):
    return pltpu.CompilerParams(
        dimension_semantics=("arbitrary",) * n_grid_axes,
        vmem_limit_bytes=VMEM_LIMIT,
    )


def _const_spec(shape):
    zeros = (0,) * len(shape)
    return pl.BlockSpec(shape, lambda *_: zeros, pipeline_mode=pl.Buffered(1))


def _proj_kernel(x_ref, gmix_ref, w_ref, gq_ref, gk_ref, bd_ref, ca_ref, sa_ref, cb_ref, sb_ref,
                 q_ref, k_ref, vt_ref):
    x = x_ref[0]
    t = x.shape[0]
    ms = jnp.mean(x * x, axis=-1, keepdims=True)
    n = (x * lax.rsqrt(ms + NORM_EPS) * gmix_ref[...]).astype(BF16)

    def rope(y, c, s):
        return y * c + pltpu.roll(y, LANES // 2, 1) * s

    def proj(col):
        y = jnp.dot(n, w_ref[:, col:col + 2 * LANES], preferred_element_type=F32)
        return y[:, :LANES], y[:, LANES:]

    ca, sa, cb, sb = ca_ref[...], sa_ref[...], cb_ref[...], sb_ref[...]
    gq, gk = gq_ref[...], gk_ref[...]

    ya = list(proj(0) + proj(2 * LANES))
    yk, yv = proj(4 * LANES)
    vt_ref[0, 0] = yv.T.astype(BF16)
    sq = jnp.concatenate([(y * y).astype(BF16) for y in ya + [yk]], axis=0)
    ms_all = jnp.dot(sq, bd_ref[...], preferred_element_type=F32)
    for p in range(A_PAIRS):
        r = lax.rsqrt(ms_all[p * t:(p + 1) * t] + NORM_EPS) * Q_SCALE
        q_ref[0, p] = (rope(ya[p] * gq, ca, sa) * r).astype(BF16)
    r = lax.rsqrt(ms_all[A_PAIRS * t:] + NORM_EPS)
    k_ref[0, 0] = (rope(yk * gk, ca, sa) * r).astype(BF16)
    for j in range(2):
        y0, y1 = proj(14 * LANES + 2 * LANES * j)
        vt_ref[0, 1 + 2 * j] = y0.T.astype(BF16)
        vt_ref[0, 2 + 2 * j] = y1.T.astype(BF16)
    for j in range(2):
        y0, y1 = proj(10 * LANES + 2 * LANES * j)
        k_ref[0, 1 + 2 * j] = rope(y0, cb, sb).astype(BF16)
        k_ref[0, 2 + 2 * j] = rope(y1, cb, sb).astype(BF16)
    for j in range(2):
        y0, y1 = proj(6 * LANES + 2 * LANES * j)
        q_ref[0, A_PAIRS + 2 * j] = (rope(y0, cb, sb) * Q_SCALE).astype(BF16)
        q_ref[0, A_PAIRS + 2 * j + 1] = (rope(y1, cb, sb) * Q_SCALE).astype(BF16)


def _proj_call(x, tables, w_in, gmix, gq, gk, bd, tile):
    b, s, d = x.shape
    ca, sa, cb, sb = tables
    grid = (b, s // tile)
    tab_spec = pl.BlockSpec((tile, LANES), lambda bi, i: (i, 0))
    return pl.pallas_call(
        _proj_kernel,
        grid=grid,
        in_specs=[
            pl.BlockSpec((1, tile, d), lambda bi, i: (bi, i, 0)),
            _const_spec((1, d)),
            _const_spec((d, IN_COLS)),
            _const_spec((1, LANES)),
            _const_spec((1, LANES)),
            _const_spec((LANES, LANES)),
            tab_spec, tab_spec, tab_spec, tab_spec,
        ],
        out_specs=[
            pl.BlockSpec((1, N_PAIRS, tile, LANES), lambda bi, i: (bi, 0, i, 0)),
            pl.BlockSpec((1, N_KV, tile, LANES), lambda bi, i: (bi, 0, i, 0)),
            pl.BlockSpec((1, N_KV, LANES, tile), lambda bi, i: (bi, 0, 0, i)),
        ],
        out_shape=[
            jax.ShapeDtypeStruct((b, N_PAIRS, s, LANES), BF16),
            jax.ShapeDtypeStruct((b, N_KV, s, LANES), BF16),
            jax.ShapeDtypeStruct((b, N_KV, LANES, s), BF16),
        ],
        compiler_params=_params(2),
        name="proj",
    )(x, gmix, w_in, gq, gk, bd, ca, sa, cb, sb)


def _attn_kernel(x_ref, q_ref, k_ref, vt_ref, km_ref, vtm_ref, lam_ref, gsub_ref, wo_ref, gffn_ref,
                 h_ref, n_ref, sa_ref, sb_ref, acca_ref, accb_ref, la_ref, lb_ref, mix_ref,
                 *, tq, key_chunk):
    n_tiles = q_ref.shape[2] // tq
    s_len = k_ref.shape[2]
    n_chunks = s_len // key_chunk
    lane = lax.broadcasted_iota(jnp.int32, (tq, LANES), 1)
    q_low = (lane & (HEAD_DIM // 2)) == 0
    low = lane < HEAD_DIM
    contract_lanes = (((1,), (1,)), ((), ()))

    lam_p = lam_ref[...]
    lam = (jnp.exp(jnp.sum(lam_p[0:1] * lam_p[1:2], axis=-1, keepdims=True))
           - jnp.exp(jnp.sum(lam_p[2:3] * lam_p[3:4], axis=-1, keepdims=True))
           + LAMBDA_INIT)

    def chunk(c):
        return slice(c * key_chunk, (c + 1) * key_chunk)

    def locate(is_b, i):
        if isinstance(i, int):
            t, j = divmod(i, A_PAIRS)
        else:
            t, j = lax.shift_right_logical(i, 2), lax.bitwise_and(i, A_PAIRS - 1)
        return t, (j + A_PAIRS if is_b else j), (j + 1 if is_b else 0)

    def stage(scores_job=None, softmax_job=None, finish_job=None):
        if scores_job is not None:
            b1, i1, buf1 = scores_job
            t1, p1, kv1 = locate(b1, i1)
            row0 = t1 * tq if isinstance(t1, int) else pl.multiple_of(t1 * tq, tq)
            qp = q_ref[0, p1, pl.ds(row0, tq), :].astype(F32)
            qq = jnp.concatenate([jnp.where(q_low, qp, 0.0), jnp.where(q_low, 0.0, qp)],
                                 axis=0).astype(BF16)
            sm1 = lax.dot_general(km_ref[0, kv1][:N_META], qq, contract_lanes,
                                  preferred_element_type=F32)
            m8 = jnp.maximum(sm1[:8], sm1[8:])
        if softmax_job is not None:
            b2, i2, buf2, (sm2, m2) = softmax_job
            _, _, kv2 = locate(b2, i2)
            pm = jnp.exp2(sm2 - m2)
            l8 = pm[:8] + pm[8:]
            pm_pad = jnp.concatenate(
                [pm.astype(BF16), jnp.zeros((META_ROWS - N_META, 2 * tq), BF16)], axis=0)
            acc = jnp.dot(vtm_ref[0, kv2], pm_pad, preferred_element_type=F32)
        out = None
        if finish_job is not None:
            b3, i3 = finish_job
            t3, p3, _ = locate(b3, i3)
            acc_ref, l_ref = (accb_ref, lb_ref) if b3 else (acca_ref, la_ref)
            inv_l = 1.0 / jnp.sum(l_ref[...], axis=0, keepdims=True)
            ot = (acc_ref[...] * inv_l).T
            o_lo, o_hi = ot[:tq], ot[tq:]
            if b3:
                dt = o_lo - lam * o_hi
                msd = jnp.mean(dt * dt, axis=-1, keepdims=True)
                out = (dt * lax.rsqrt(msd + NORM_EPS) * gsub_ref[...]
                       * (1.0 - LAMBDA_INIT)).astype(BF16)
            else:
                out = jnp.where(low, o_lo, o_hi).astype(BF16)
            mix_ref[t3, p3] = out
        for c in range(n_chunks):
            if scores_job is not None:
                sc = lax.dot_general(k_ref[0, kv1, chunk(c), :], qq, contract_lanes,
                                     preferred_element_type=F32)
                buf1[chunk(c), :] = sc
                m8 = jnp.maximum(m8, jnp.max(sc.reshape(key_chunk // 8, 8, 2 * tq), axis=0))
            if softmax_job is not None:
                pc = jnp.exp2(buf2[chunk(c), :] - m2)
                l8 = l8 + jnp.sum(pc.reshape(key_chunk // 8, 8, 2 * tq), axis=0)
                acc = acc + jnp.dot(vt_ref[0, kv2, :, chunk(c)], pc.astype(BF16),
                                    preferred_element_type=F32)
        if softmax_job is not None:
            acc_ref, l_ref = (accb_ref, lb_ref) if b2 else (acca_ref, la_ref)
            acc_ref[...] = acc
            l_ref[...] = l8
        if scores_job is not None:
            return sm1, jnp.max(m8, axis=0, keepdims=True)
        return out

    grp_a, grp_b = False, True
    last = n_tiles * A_PAIRS - 1
    stats = stage(scores_job=(grp_a, 0, sa_ref))
    stats = stage(scores_job=(grp_b, 0, sb_ref), softmax_job=(grp_a, 0, sa_ref, stats))
    stats = stage(scores_job=(grp_a, 1, sa_ref), softmax_job=(grp_b, 0, sb_ref, stats),
                  finish_job=(grp_a, 0))

    def body(i, stats_a):
        stats_b = stage(scores_job=(grp_b, i, sb_ref), softmax_job=(grp_a, i, sa_ref, stats_a),
                        finish_job=(grp_b, i - 1))
        return stage(scores_job=(grp_a, i + 1, sa_ref), softmax_job=(grp_b, i, sb_ref, stats_b),
                     finish_job=(grp_a, i))

    stats = lax.fori_loop(0, (last - 1) // 2,
                          lambda k, st: body(2 * k + 2, body(2 * k + 1, st)), stats)
    stats = stage(scores_job=(grp_b, last, sb_ref), softmax_job=(grp_a, last, sa_ref, stats),
                  finish_job=(grp_b, last - 1))
    stage(softmax_job=(grp_b, last, sb_ref, stats), finish_job=(grp_a, last))
    out_last = stage(finish_job=(grp_b, last))

    split = (N_PAIRS - 1) * LANES
    for t in range(n_tiles):
        rows = slice(t * tq, (t + 1) * tq)
        if t + 1 < n_tiles:
            mix = jnp.concatenate([mix_ref[t, p] for p in range(N_PAIRS)], axis=1)
            h = x_ref[0, rows, :] + jnp.dot(mix, wo_ref[...], preferred_element_type=F32)
        else:
            mix = jnp.concatenate([mix_ref[t, p] for p in range(N_PAIRS - 1)], axis=1)
            h = (x_ref[0, rows, :]
                 + jnp.dot(mix, wo_ref[:split, :], preferred_element_type=F32)
                 + jnp.dot(out_last, wo_ref[split:, :], preferred_element_type=F32))
        h_ref[0, rows, :] = h
        ms = jnp.mean(h * h, axis=-1, keepdims=True)
        n_ref[0, rows, :] = (h * lax.rsqrt(ms + NORM_EPS) * gffn_ref[...]).astype(BF16)


def _attn_call(x, q, k, vt, km, vtm, lam_p, gsub, w_out, g_ffn, tq, n_tiles, vt_buffers, key_chunk,
               shared_q):
    _, _, sq, _ = q.shape
    b, _, s, _ = k.shape
    d = x.shape[-1]
    rows = tq * n_tiles
    grid = (b, sq // rows)
    if shared_q:
        q_map = lambda bi, i: (0, 0, i, 0)
        x_map = lambda bi, i: (0, i, 0)
    else:
        q_map = lambda bi, i: (bi, 0, i, 0)
        x_map = lambda bi, i: (bi, i, 0)
    return pl.pallas_call(
        functools.partial(_attn_kernel, tq=tq, key_chunk=key_chunk),
        grid=grid,
        in_specs=[
            pl.BlockSpec((1, rows, d), x_map),
            pl.BlockSpec((1, N_PAIRS, rows, LANES), q_map),
            pl.BlockSpec((1, N_KV, s, LANES), lambda bi, i: (bi, 0, 0, 0)),
            pl.BlockSpec((1, N_KV, LANES, s), lambda bi, i: (bi, 0, 0, 0),
                         pipeline_mode=pl.Buffered(vt_buffers)),
            _const_spec((1, N_KV, META_ROWS, LANES)),
            _const_spec((1, N_KV, LANES, META_ROWS)),
            _const_spec((8, LANES)),
            _const_spec((1, LANES)),
            _const_spec((d, d)),
            _const_spec((1, d)),
        ],
        out_specs=[
            pl.BlockSpec((1, rows, d), lambda bi, i: (bi, i, 0)),
            pl.BlockSpec((1, rows, d), lambda bi, i: (bi, i, 0)),
        ],
        out_shape=[
            jax.ShapeDtypeStruct((b, sq, d), F32),
            jax.ShapeDtypeStruct((b, sq, d), BF16),
        ],
        scratch_shapes=[
            pltpu.VMEM((s, 2 * tq), F32),
            pltpu.VMEM((s, 2 * tq), F32),
            pltpu.VMEM((LANES, 2 * tq), F32),
            pltpu.VMEM((LANES, 2 * tq), F32),
            pltpu.VMEM((8, 2 * tq), F32),
            pltpu.VMEM((8, 2 * tq), F32),
            pltpu.VMEM((n_tiles, N_PAIRS, tq, LANES), BF16),
        ],
        compiler_params=_params(2),
        name="attn",
    )(x, q, k, vt, km, vtm, lam_p, gsub, w_out, g_ffn)


def _ffn_kernel(h_ref, n_ref, left_ref, meta_ref, right_ref, wg_ref, wu_ref, cw_ref, wd_ref,
                gfin_ref, o_ref, gate0_ref, gate1_ref, u_ref):
    i = pl.program_id(1)
    last = pl.num_programs(1) - 1
    t = n_ref.shape[1]
    n = n_ref[0]
    left = jnp.where(i == 0, meta_ref[0], left_ref[0])
    right = jnp.where(i == last, jnp.zeros_like(right_ref[0]), right_ref[0])
    n_ext = jnp.concatenate([left, n, right], axis=0)

    for c in range(N_FF_CHUNKS):
        gate_ref = gate1_ref if c % 2 else gate0_ref
        cols = slice(c * FF_CHUNK, (c + 1) * FF_CHUNK)
        gate_ref[...] = jnp.dot(n_ext, wg_ref[:, cols], preferred_element_type=F32)
        cw = cw_ref[:, cols]
        g = (gate_ref[HALO - 1:HALO - 1 + t, :] * cw[0:1]
             + gate_ref[HALO:HALO + t, :] * cw[1:2]
             + gate_ref[HALO + 1:HALO + 1 + t, :] * cw[2:3]
             + cw[3:4])
        up = jnp.dot(n, wu_ref[:, cols], preferred_element_type=F32)
        act = 0.5 * g * (1.0 + lax.erf(g * (2.0 ** -0.5)))
        u_ref[:, cols] = (act * up).astype(BF16)
    y = h_ref[0] + jnp.dot(u_ref[...], wd_ref[...], preferred_element_type=F32)
    ms = jnp.mean(y * y, axis=-1, keepdims=True)
    o_ref[0] = y * lax.rsqrt(ms + NORM_EPS) * gfin_ref[...]


def _ffn_call(h, n2, n2_meta, wg, wu, cw, wd, g_final, tile):
    b, s, d = h.shape
    grid = (b, s // tile)
    per = tile // HALO
    n_halo_blocks = s // HALO
    return pl.pallas_call(
        _ffn_kernel,
        grid=grid,
        in_specs=[
            pl.BlockSpec((1, tile, d), lambda bi, i: (bi, i, 0)),
            pl.BlockSpec((1, tile, d), lambda bi, i: (bi, i, 0)),
            pl.BlockSpec((1, HALO, d), lambda bi, i: (bi, jnp.maximum(i * per - 1, 0), 0)),
            pl.BlockSpec((1, HALO, d), lambda bi, i: (bi, 0, 0)),
            pl.BlockSpec((1, HALO, d),
                         lambda bi, i: (bi, jnp.minimum((i + 1) * per, n_halo_blocks - 1), 0)),
            _const_spec((d, D_FF)),
            _const_spec((d, D_FF)),
            _const_spec((8, D_FF)),
            _const_spec((D_FF, d)),
            _const_spec((1, d)),
        ],
        out_specs=pl.BlockSpec((1, tile, d), lambda bi, i: (bi, i, 0)),
        out_shape=jax.ShapeDtypeStruct((b, s, d), F32),
        scratch_shapes=[
            pltpu.VMEM((tile + 2 * HALO, FF_CHUNK), F32),
            pltpu.VMEM((tile + 2 * HALO, FF_CHUNK), F32),
            pltpu.VMEM((tile, D_FF), BF16),
        ],
        compiler_params=_params(2),
        name="ffn",
    )(h, n2, n2, n2_meta, n2, wg, wu, cw, wd, g_final)


def _pair_layout(ang):
    c = jnp.cos(ang)
    s = jnp.sin(ang)
    return jnp.tile(c, (1, 4)), jnp.concatenate([-s, -s, s, s], axis=-1)


def _linear_inv():
    return ROPE_THETA ** (-jnp.arange(0, HEAD_DIM, 2, dtype=F32) / HEAD_DIM)


def _real_tables(s):
    t = jnp.arange(s)
    rowp = (t // GRID_W).astype(F32)
    colp = (t % GRID_W).astype(F32)
    axis_dim = HEAD_DIM // 2
    inv_a = ROPE_THETA ** (-jnp.arange(0, axis_dim, 2, dtype=F32) / axis_dim)
    ang_a = jnp.concatenate([rowp[:, None] * inv_a[None], colp[:, None] * inv_a[None]], axis=-1)
    pos = jnp.arange(N_META + s, dtype=F32)[N_META:]
    ang_b = pos[:, None] * _linear_inv()[None]
    return _pair_layout(ang_a) + _pair_layout(ang_b)


def _meta_tables():
    ang_a = jnp.zeros((META_ROWS, HEAD_DIM // 2), F32)
    pos = jnp.arange(META_ROWS, dtype=F32)
    ang_b = pos[:, None] * _linear_inv()[None]
    return _pair_layout(ang_a) + _pair_layout(ang_b)


def _attn_step_plan(s, tq):
    kv_bytes = N_KV * s * LANES * 2
    per_tile = tq * D_MODEL * (2 * 2 * 4 + 2 * 2 * 2 + 2)

    def tiles(vt_buffers):
        fixed = 2 * s * 2 * tq * 4 + (2 + vt_buffers) * kv_bytes + ATTN_TEMP_BYTES
        n = 1
        while 2 * n * tq <= s and fixed + 2 * n * per_tile <= VMEM_LIMIT:
            n *= 2
        return n

    return (tiles(1), 1) if tiles(1) > tiles(2) else (tiles(2), 2)


def _trunk(x, meta_x, meta_qkv, prm, cfg):
    s = x.shape[1]
    q, k, vt = _proj_call(x, _real_tables(s), prm["w_in"], prm["g_mix"], prm["gq"], prm["gk"],
                          prm["bd"], min(s, cfg["proj_tile"]))
    q_m, k_m, vt_m = meta_qkv
    kc = min(s, cfg["key_chunk"])
    tq = min(s, cfg["tq"])
    n_tiles, vt_buffers = _attn_step_plan(s, tq)
    h1, n2 = _attn_call(x, q, k, vt, k_m, vt_m, prm["lam"], prm["g_subln"], prm["w_out"],
                        prm["g_ffn"], tq, n_tiles, vt_buffers, kc, shared_q=False)
    _, n2_m = _attn_call(meta_x[:, :META_Q], q_m[:, :, :META_Q], k, vt, k_m, vt_m, prm["lam"],
                         prm["g_subln"], prm["w_out"], prm["g_ffn"], META_Q, 1, 2, kc,
                         shared_q=True)
    return _ffn_call(h1, n2, n2_m, prm["wg"], prm["wu"], prm["cw"], prm["wd"], prm["g_final"],
                     min(s, cfg["row_tile"]))


_CFG = dict(proj_tile=1024, tq=256, key_chunk=512, row_tile=1024)


def kernel(x_prompt, x_sample, meta_tokens, g_mix, w_in, g_qnorm_a, g_knorm_a, lambda_q1, lambda_k1,
           lambda_q2, lambda_k2, g_subln, w_out, g_ffn, w_ff_gate, w_ff_up, conv_w, conv_b,
           w_ff_down, g_final):
    assert w_in.shape[0] == 1, "single-layer trunk"
    d = D_MODEL
    head_order = np.array([0, 4, 1, 5, 2, 6, 3, 7])
    a_cols = (head_order[:, None] * HEAD_DIM + np.arange(HEAD_DIM)[None]).reshape(-1)
    in_perm = np.concatenate([a_cols, np.arange(A_PAIRS * LANES, IN_COLS)])
    out_perm = np.concatenate([a_cols, np.arange(A_PAIRS * LANES, d)])
    half = HEAD_DIM // 2
    tile_perm = np.concatenate([np.arange(0, half), np.arange(2 * half, 3 * half),
                                np.arange(half, 2 * half), np.arange(3 * half, 4 * half)])
    qk_tiles = list(range(0, 5)) + list(range(6, 14))
    for tile in qk_tiles:
        in_perm[tile * LANES:(tile + 1) * LANES] = in_perm[tile * LANES + tile_perm]
    lane_dim = np.concatenate([np.arange(half), np.arange(half),
                               np.arange(half, 2 * half), np.arange(half, 2 * half)])
    blk = (np.arange(LANES) // half) % 2
    cw = jnp.concatenate([conv_w[0], conv_b[0][None], jnp.zeros((4, D_FF), F32)], axis=0)
    lam = jnp.stack([lambda_q1[0], lambda_k1[0], lambda_q2[0], lambda_k2[0]])
    prm = dict(
        w_in=w_in[0][:, in_perm].astype(BF16),
        g_mix=g_mix[0][None],
        gq=g_qnorm_a[0][lane_dim][None],
        gk=g_knorm_a[0][lane_dim][None],
        bd=jnp.asarray((blk[:, None] == blk[None]) / HEAD_DIM, BF16),
        lam=jnp.zeros((8, LANES), F32).at[:4, :HEAD_DIM].set(lam),
        g_subln=g_subln[0][None],
        w_out=w_out[0][out_perm].astype(BF16),
        g_ffn=g_ffn[0][None],
        wg=w_ff_gate[0].astype(BF16),
        wu=w_ff_up[0].astype(BF16),
        cw=cw,
        wd=w_ff_down[0].astype(BF16),
        g_final=g_final[None],
    )
    meta_x = jnp.zeros((1, META_ROWS, d), F32).at[0, :N_META].set(meta_tokens)
    meta_qkv = _proj_call(meta_x, _meta_tables(), prm["w_in"], prm["g_mix"], prm["gq"], prm["gk"],
                          prm["bd"], META_ROWS)
    y_prompt = _trunk(x_prompt, meta_x, meta_qkv, prm, _CFG)
    y_sample = _trunk(x_sample, meta_x, meta_qkv, prm, _CFG)
    return (y_prompt, y_sample)
```

```python
import functools
import math

import jax
import jax.numpy as jnp
import numpy as np
from jax import lax
from jax.experimental import pallas as pl
from jax.experimental.pallas import tpu as pltpu

F32 = jnp.float32
BF16 = jnp.bfloat16

D_MODEL = 1024
HEAD_DIM = 64
N_META = 16
GRID_W = 64
ROPE_THETA = 10000.0
NORM_EPS = 1e-6
LANES = 128
N_PAIRS = 8
N_KV = 5
A_PAIRS = 4
D_FF = 2816
FF_CHUNK = 256
N_FF_CHUNKS = D_FF // FF_CHUNK
IN_COLS = 2304
HALO = 16
META_ROWS = 128
META_Q = 64
LAMBDA_INIT = 0.8 - 0.6 * math.exp(-0.3 * 0)
Q_SCALE = (HEAD_DIM ** -0.5) * math.log2(math.e)
VMEM_LIMIT = 56 * 1024 * 1024
ATTN_TEMP_BYTES = 11 * 1024 * 1024


def _params(n_grid_axes):
    return pltpu.CompilerParams(
        dimension_semantics=("arbitrary",) * n_grid_axes,
        vmem_limit_bytes=VMEM_LIMIT,
    )


def _const_spec(shape):
    zeros = (0,) * len(shape)
    return pl.BlockSpec(shape, lambda *_: zeros, pipeline_mode=pl.Buffered(1))


def _proj_kernel(x_ref, gmix_ref, w_ref, gq_ref, gk_ref, bd_ref, ca_ref, sa_ref, cb_ref, sb_ref,
                 q_ref, k_ref, vt_ref):
    x = x_ref[0]
    t = x.shape[0]
    ms = jnp.mean(x * x, axis=-1, keepdims=True)
    n = (x * lax.rsqrt(ms + NORM_EPS) * gmix_ref[...]).astype(BF16)

    def rope(y, c, s):
        return y * c + pltpu.roll(y, LANES // 2, 1) * s

    def proj(col):
        y = jnp.dot(n, w_ref[:, col:col + 2 * LANES], preferred_element_type=F32)
        return y[:, :LANES], y[:, LANES:]

    ca, sa, cb, sb = ca_ref[...], sa_ref[...], cb_ref[...], sb_ref[...]
    gq, gk = gq_ref[...], gk_ref[...]

    ya = list(proj(0) + proj(2 * LANES))
    yk, yv = proj(4 * LANES)
    vt_ref[0, 0] = yv.T.astype(BF16)
    sq = jnp.concatenate([(y * y).astype(BF16) for y in ya + [yk]], axis=0)
    ms_all = jnp.dot(sq, bd_ref[...], preferred_element_type=F32)
    for p in range(A_PAIRS):
        r = lax.rsqrt(ms_all[p * t:(p + 1) * t] + NORM_EPS) * Q_SCALE
        q_ref[0, p] = (rope(ya[p] * gq, ca, sa) * r).astype(BF16)
    r = lax.rsqrt(ms_all[A_PAIRS * t:] + NORM_EPS)
    k_ref[0, 0] = (rope(yk * gk, ca, sa) * r).astype(BF16)
    for j in range(2):
        y0, y1 = proj(14 * LANES + 2 * LANES * j)
        vt_ref[0, 1 + 2 * j] = y0.T.astype(BF16)
        vt_ref[0, 2 + 2 * j] = y1.T.astype(BF16)
    for j in range(2):
        y0, y1 = proj(10 * LANES + 2 * LANES * j)
        k_ref[0, 1 + 2 * j] = rope(y0, cb, sb).astype(BF16)
        k_ref[0, 2 + 2 * j] = rope(y1, cb, sb).astype(BF16)
    for j in range(2):
        y0, y1 = proj(6 * LANES + 2 * LANES * j)
        q_ref[0, A_PAIRS + 2 * j] = (rope(y0, cb, sb) * Q_SCALE).astype(BF16)
        q_ref[0, A_PAIRS + 2 * j + 1] = (rope(y1, cb, sb) * Q_SCALE).astype(BF16)


def _proj_call(x, tables, w_in, gmix, gq, gk, bd, tile):
    b, s, d = x.shape
    ca, sa, cb, sb = tables
    grid = (b, s // tile)
    tab_spec = pl.BlockSpec((tile, LANES), lambda bi, i: (i, 0))
    return pl.pallas_call(
        _proj_kernel,
        grid=grid,
        in_specs=[
            pl.BlockSpec((1, tile, d), lambda bi, i: (bi, i, 0)),
            _const_spec((1, d)),
            _const_spec((d, IN_COLS)),
            _const_spec((1, LANES)),
            _const_spec((1, LANES)),
            _const_spec((LANES, LANES)),
            tab_spec, tab_spec, tab_spec, tab_spec,
        ],
        out_specs=[
            pl.BlockSpec((1, N_PAIRS, tile, LANES), lambda bi, i: (bi, 0, i, 0)),
            pl.BlockSpec((1, N_KV, tile, LANES), lambda bi, i: (bi, 0, i, 0)),
            pl.BlockSpec((1, N_KV, LANES, tile), lambda bi, i: (bi, 0, 0, i)),
        ],
        out_shape=[
            jax.ShapeDtypeStruct((b, N_PAIRS, s, LANES), BF16),
            jax.ShapeDtypeStruct((b, N_KV, s, LANES), BF16),
            jax.ShapeDtypeStruct((b, N_KV, LANES, s), BF16),
        ],
        compiler_params=_params(2),
        name="proj",
    )(x, gmix, w_in, gq, gk, bd, ca, sa, cb, sb)


def _attn_kernel(x_ref, q_ref, k_ref, vt_ref, km_ref, vtm_ref, lam_ref, gsub_ref, wo_ref, gffn_ref,
                 h_ref, n_ref, sa_ref, sb_ref, acca_ref, accb_ref, la_ref, lb_ref, mix_ref,
                 *, tq, key_chunk):
    n_tiles = q_ref.shape[2] // tq
    s_len = k_ref.shape[2]
    n_chunks = s_len // key_chunk
    lane = lax.broadcasted_iota(jnp.int32, (tq, LANES), 1)
    q_low = (lane & (HEAD_DIM // 2)) == 0
    low = lane < HEAD_DIM
    contract_lanes = (((1,), (1,)), ((), ()))

    lam_p = lam_ref[...]
    lam = (jnp.exp(jnp.sum(lam_p[0:1] * lam_p[1:2], axis=-1, keepdims=True))
           - jnp.exp(jnp.sum(lam_p[2:3] * lam_p[3:4], axis=-1, keepdims=True))
           + LAMBDA_INIT)

    def chunk(c):
        return slice(c * key_chunk, (c + 1) * key_chunk)

    def locate(is_b, i):
        if isinstance(i, int):
            t, j = divmod(i, A_PAIRS)
        else:
            t, j = lax.shift_right_logical(i, 2), lax.bitwise_and(i, A_PAIRS - 1)
        return t, (j + A_PAIRS if is_b else j), (j + 1 if is_b else 0)

    def stage(scores_job=None, softmax_job=None, finish_job=None):
        if scores_job is not None:
            b1, i1, buf1 = scores_job
            t1, p1, kv1 = locate(b1, i1)
            row0 = t1 * tq if isinstance(t1, int) else pl.multiple_of(t1 * tq, tq)
            qp = q_ref[0, p1, pl.ds(row0, tq), :].astype(F32)
            qq = jnp.concatenate([jnp.where(q_low, qp, 0.0), jnp.where(q_low, 0.0, qp)],
                                 axis=0).astype(BF16)
            sm1 = lax.dot_general(km_ref[0, kv1][:N_META], qq, contract_lanes,
                                  preferred_element_type=F32)
            m8 = jnp.maximum(sm1[:8], sm1[8:])
        if softmax_job is not None:
            b2, i2, buf2, (sm2, m2) = softmax_job
            _, _, kv2 = locate(b2, i2)
            pm = jnp.exp2(sm2 - m2)
            l8 = pm[:8] + pm[8:]
            pm_pad = jnp.concatenate(
                [pm.astype(BF16), jnp.zeros((META_ROWS - N_META, 2 * tq), BF16)], axis=0)
            acc = jnp.dot(vtm_ref[0, kv2], pm_pad, preferred_element_type=F32)
        out = None
        if finish_job is not None:
            b3, i3 = finish_job
            t3, p3, _ = locate(b3, i3)
            acc_ref, l_ref = (accb_ref, lb_ref) if b3 else (acca_ref, la_ref)
            inv_l = 1.0 / jnp.sum(l_ref[...], axis=0, keepdims=True)
            ot = (acc_ref[...] * inv_l).T
            o_lo, o_hi = ot[:tq], ot[tq:]
            if b3:
                dt = o_lo - lam * o_hi
                msd = jnp.mean(dt * dt, axis=-1, keepdims=True)
                out = (dt * lax.rsqrt(msd + NORM_EPS) * gsub_ref[...]
                       * (1.0 - LAMBDA_INIT)).astype(BF16)
            else:
                out = jnp.where(low, o_lo, o_hi).astype(BF16)
            mix_ref[t3, p3] = out
        for c in range(n_chunks):
            if scores_job is not None:
                sc = lax.dot_general(k_ref[0, kv1, chunk(c), :], qq, contract_lanes,
                                     preferred_element_type=F32)
                buf1[chunk(c), :] = sc
                m8 = jnp.maximum(m8, jnp.max(sc.reshape(key_chunk // 8, 8, 2 * tq), axis=0))
            if softmax_job is not None:
                pc = jnp.exp2(buf2[chunk(c), :] - m2)
                l8 = l8 + jnp.sum(pc.reshape(key_chunk // 8, 8, 2 * tq), axis=0)
                acc = acc + jnp.dot(vt_ref[0, kv2, :, chunk(c)], pc.astype(BF16),
                                    preferred_element_type=F32)
        if softmax_job is not None:
            acc_ref, l_ref = (accb_ref, lb_ref) if b2 else (acca_ref, la_ref)
            acc_ref[...] = acc
            l_ref[...] = l8
        if scores_job is not None:
            return sm1, jnp.max(m8, axis=0, keepdims=True)
        return out

    grp_a, grp_b = False, True
    last = n_tiles * A_PAIRS - 1
    stats = stage(scores_job=(grp_a, 0, sa_ref))
    stats = stage(scores_job=(grp_b, 0, sb_ref), softmax_job=(grp_a, 0, sa_ref, stats))
    stats = stage(scores_job=(grp_a, 1, sa_ref), softmax_job=(grp_b, 0, sb_ref, stats),
                  finish_job=(grp_a, 0))

    def body(i, stats_a):
        stats_b = stage(scores_job=(grp_b, i, sb_ref), softmax_job=(grp_a, i, sa_ref, stats_a),
                        finish_job=(grp_b, i - 1))
        return stage(scores_job=(grp_a, i + 1, sa_ref), softmax_job=(grp_b, i, sb_ref, stats_b),
                     finish_job=(grp_a, i))

    stats = lax.fori_loop(0, (last - 1) // 2,
                          lambda k, st: body(2 * k + 2, body(2 * k + 1, st)), stats)
    stats = stage(scores_job=(grp_b, last, sb_ref), softmax_job=(grp_a, last, sa_ref, stats),
                  finish_job=(grp_b, last - 1))
    stage(softmax_job=(grp_b, last, sb_ref, stats), finish_job=(grp_a, last))
    out_last = stage(finish_job=(grp_b, last))

    split = (N_PAIRS - 1) * LANES
    for t in range(n_tiles):
        rows = slice(t * tq, (t + 1) * tq)
        if t + 1 < n_tiles:
            mix = jnp.concatenate([mix_ref[t, p] for p in range(N_PAIRS)], axis=1)
            h = x_ref[0, rows, :] + jnp.dot(mix, wo_ref[...], preferred_element_type=F32)
        else:
            mix = jnp.concatenate([mix_ref[t, p] for p in range(N_PAIRS - 1)], axis=1)
            h = (x_ref[0, rows, :]
                 + jnp.dot(mix, wo_ref[:split, :], preferred_element_type=F32)
                 + jnp.dot(out_last, wo_ref[split:, :], preferred_element_type=F32))
        h_ref[0, rows, :] = h
        ms = jnp.mean(h * h, axis=-1, keepdims=True)
        n_ref[0, rows, :] = (h * lax.rsqrt(ms + NORM_EPS) * gffn_ref[...]).astype(BF16)


def _attn_call(x, q, k, vt, km, vtm, lam_p, gsub, w_out, g_ffn, tq, n_tiles, vt_buffers, key_chunk,
               shared_q):
    _, _, sq, _ = q.shape
    b, _, s, _ = k.shape
    d = x.shape[-1]
    rows = tq * n_tiles
    grid = (b, sq // rows)
    if shared_q:
        q_map = lambda bi, i: (0, 0, i, 0)
        x_map = lambda bi, i: (0, i, 0)
    else:
        q_map = lambda bi, i: (bi, 0, i, 0)
        x_map = lambda bi, i: (bi, i, 0)
    return pl.pallas_call(
        functools.partial(_attn_kernel, tq=tq, key_chunk=key_chunk),
        grid=grid,
        in_specs=[
            pl.BlockSpec((1, rows, d), x_map),
            pl.BlockSpec((1, N_PAIRS, rows, LANES), q_map),
            pl.BlockSpec((1, N_KV, s, LANES), lambda bi, i: (bi, 0, 0, 0)),
            pl.BlockSpec((1, N_KV, LANES, s), lambda bi, i: (bi, 0, 0, 0),
                         pipeline_mode=pl.Buffered(vt_buffers)),
            _const_spec((1, N_KV, META_ROWS, LANES)),
            _const_spec((1, N_KV, LANES, META_ROWS)),
            _const_spec((8, LANES)),
            _const_spec((1, LANES)),
            _const_spec((d, d)),
            _const_spec((1, d)),
        ],
        out_specs=[
            pl.BlockSpec((1, rows, d), lambda bi, i: (bi, i, 0)),
            pl.BlockSpec((1, rows, d), lambda bi, i: (bi, i, 0)),
        ],
        out_shape=[
            jax.ShapeDtypeStruct((b, sq, d), F32),
            jax.ShapeDtypeStruct((b, sq, d), BF16),
        ],
        scratch_shapes=[
            pltpu.VMEM((s, 2 * tq), F32),
            pltpu.VMEM((s, 2 * tq), F32),
            pltpu.VMEM((LANES, 2 * tq), F32),
            pltpu.VMEM((LANES, 2 * tq), F32),
            pltpu.VMEM((8, 2 * tq), F32),
            pltpu.VMEM((8, 2 * tq), F32),
            pltpu.VMEM((n_tiles, N_PAIRS, tq, LANES), BF16),
        ],
        compiler_params=_params(2),
        name="attn",
    )(x, q, k, vt, km, vtm, lam_p, gsub, w_out, g_ffn)


def _meta_attn_kernel(x_ref, q_ref, k_ref, vt_ref, km_ref, vtm_ref, lam_ref, gsub_ref, wo_ref,
                      gffn_ref, n_ref, s0_ref, s1_ref, *, key_chunk):
    s_len = k_ref.shape[2]
    n_chunks = s_len // key_chunk
    lane = lax.broadcasted_iota(jnp.int32, (N_META, LANES), 1)
    q_low = (lane & (HEAD_DIM // 2)) == 0
    low = lane < HEAD_DIM
    contract_lanes = (((1,), (1,)), ((), ()))

    lam_p = lam_ref[...]
    lam = (jnp.exp(jnp.sum(lam_p[0:1] * lam_p[1:2], axis=-1, keepdims=True))
           - jnp.exp(jnp.sum(lam_p[2:3] * lam_p[3:4], axis=-1, keepdims=True))
           + LAMBDA_INIT)

    def maps(p):
        qp = q_ref[0, p].astype(F32)
        return [jnp.where(q_low, qp, 0.0), jnp.where(q_low, 0.0, qp)]

    def attend(q_rows, kv, s_ref):
        qq = q_rows.astype(BF16)
        sm = lax.dot_general(km_ref[0, kv][:N_META], qq, contract_lanes,
                             preferred_element_type=F32)
        m8 = jnp.maximum(sm[:8], sm[8:])
        for c in range(n_chunks):
            rows = slice(c * key_chunk, (c + 1) * key_chunk)
            sc = lax.dot_general(k_ref[0, kv, rows, :], qq, contract_lanes,
                                 preferred_element_type=F32)
            s_ref[rows, :] = sc
            m8 = jnp.maximum(m8, jnp.max(sc.reshape(key_chunk // 8, 8, LANES), axis=0))
        m = jnp.max(m8, axis=0, keepdims=True)
        pm = jnp.exp2(sm - m)
        l8 = pm[:8] + pm[8:]
        pm_pad = jnp.concatenate(
            [pm.astype(BF16), jnp.zeros((META_ROWS - N_META, LANES), BF16)], axis=0)
        acc = jnp.dot(vtm_ref[0, kv], pm_pad, preferred_element_type=F32)
        for c in range(n_chunks):
            rows = slice(c * key_chunk, (c + 1) * key_chunk)
            pc = jnp.exp2(s_ref[rows, :] - m)
            l8 = l8 + jnp.sum(pc.reshape(key_chunk // 8, 8, LANES), axis=0)
            acc = acc + jnp.dot(vt_ref[0, kv, :, rows], pc.astype(BF16),
                                preferred_element_type=F32)
        return (acc * (1.0 / jnp.sum(l8, axis=0, keepdims=True))).T

    pieces = []
    ot = attend(jnp.concatenate([mp for p in range(A_PAIRS) for mp in maps(p)], axis=0), 0, s0_ref)
    for p in range(A_PAIRS):
        o_lo = ot[2 * N_META * p:2 * N_META * p + N_META]
        o_hi = ot[2 * N_META * p + N_META:2 * N_META * (p + 1)]
        pieces.append(jnp.where(low, o_lo, o_hi).astype(BF16))
    idle = jnp.zeros((LANES - 2 * N_META, LANES), F32)
    for hd in range(N_PAIRS - A_PAIRS):
        ot = attend(jnp.concatenate(maps(A_PAIRS + hd) + [idle], axis=0), hd + 1,
                    s1_ref if hd % 2 == 0 else s0_ref)
        dt = ot[:N_META] - lam * ot[N_META:2 * N_META]
        msd = jnp.mean(dt * dt, axis=-1, keepdims=True)
        pieces.append((dt * lax.rsqrt(msd + NORM_EPS) * gsub_ref[...]
                       * (1.0 - LAMBDA_INIT)).astype(BF16))
    mix = jnp.concatenate(pieces, axis=1)
    h = x_ref[0] + jnp.dot(mix, wo_ref[...], preferred_element_type=F32)
    ms = jnp.mean(h * h, axis=-1, keepdims=True)
    n_ref[0] = (h * lax.rsqrt(ms + NORM_EPS) * gffn_ref[...]).astype(BF16)


def _meta_attn_call(x_meta, q_meta, k, vt, km, vtm, lam_p, gsub, w_out, g_ffn, key_chunk):
    b, _, s, _ = k.shape
    d = x_meta.shape[-1]
    return pl.pallas_call(
        functools.partial(_meta_attn_kernel, key_chunk=key_chunk),
        grid=(b,),
        in_specs=[
            _const_spec((1, N_META, d)),
            _const_spec((1, N_PAIRS, N_META, LANES)),
            pl.BlockSpec((1, N_KV, s, LANES), lambda bi: (bi, 0, 0, 0)),
            pl.BlockSpec((1, N_KV, LANES, s), lambda bi: (bi, 0, 0, 0)),
            _const_spec((1, N_KV, META_ROWS, LANES)),
            _const_spec((1, N_KV, LANES, META_ROWS)),
            _const_spec((8, LANES)),
            _const_spec((1, LANES)),
            _const_spec((d, d)),
            _const_spec((1, d)),
        ],
        out_specs=pl.BlockSpec((1, N_META, d), lambda bi: (bi, 0, 0)),
        out_shape=jax.ShapeDtypeStruct((b, N_META, d), BF16),
        scratch_shapes=[pltpu.VMEM((s, LANES), F32), pltpu.VMEM((s, LANES), F32)],
        compiler_params=_params(1),
        name="meta_attn",
    )(x_meta, q_meta, k, vt, km, vtm, lam_p, gsub, w_out, g_ffn)


def _ffn_kernel(h_ref, n_ref, left_ref, meta_ref, right_ref, wg_ref, wu_ref, cw_ref, wd_ref,
                gfin_ref, o_ref, gate0_ref, gate1_ref, u_ref):
    i = pl.program_id(1)
    last = pl.num_programs(1) - 1
    t = n_ref.shape[1]
    n = n_ref[0]
    left = jnp.where(i == 0, meta_ref[0], left_ref[0])
    right = jnp.where(i == last, jnp.zeros_like(right_ref[0]), right_ref[0])
    n_ext = jnp.concatenate([left, n, right], axis=0)

    for c in range(N_FF_CHUNKS):
        gate_ref = gate1_ref if c % 2 else gate0_ref
        cols = slice(c * FF_CHUNK, (c + 1) * FF_CHUNK)
        gate_ref[...] = jnp.dot(n_ext, wg_ref[:, cols], preferred_element_type=F32)
        cw = cw_ref[:, cols]
        g = (gate_ref[HALO - 1:HALO - 1 + t, :] * cw[0:1]
             + gate_ref[HALO:HALO + t, :] * cw[1:2]
             + gate_ref[HALO + 1:HALO + 1 + t, :] * cw[2:3]
             + cw[3:4])
        up = jnp.dot(n, wu_ref[:, cols], preferred_element_type=F32)
        act = 0.5 * g * (1.0 + lax.erf(g * (2.0 ** -0.5)))
        u_ref[:, cols] = (act * up).astype(BF16)
    y = h_ref[0] + jnp.dot(u_ref[...], wd_ref[...], preferred_element_type=F32)
    ms = jnp.mean(y * y, axis=-1, keepdims=True)
    o_ref[0] = y * lax.rsqrt(ms + NORM_EPS) * gfin_ref[...]


def _ffn_call(h, n2, n2_meta, wg, wu, cw, wd, g_final, tile):
    b, s, d = h.shape
    grid = (b, s // tile)
    per = tile // HALO
    n_halo_blocks = s // HALO
    return pl.pallas_call(
        _ffn_kernel,
        grid=grid,
        in_specs=[
            pl.BlockSpec((1, tile, d), lambda bi, i: (bi, i, 0)),
            pl.BlockSpec((1, tile, d), lambda bi, i: (bi, i, 0)),
            pl.BlockSpec((1, HALO, d), lambda bi, i: (bi, jnp.maximum(i * per - 1, 0), 0)),
            pl.BlockSpec((1, HALO, d), lambda bi, i: (bi, 0, 0)),
            pl.BlockSpec((1, HALO, d),
                         lambda bi, i: (bi, jnp.minimum((i + 1) * per, n_halo_blocks - 1), 0)),
            _const_spec((d, D_FF)),
            _const_spec((d, D_FF)),
            _const_spec((8, D_FF)),
            _const_spec((D_FF, d)),
            _const_spec((1, d)),
        ],
        out_specs=pl.BlockSpec((1, tile, d), lambda bi, i: (bi, i, 0)),
        out_shape=jax.ShapeDtypeStruct((b, s, d), F32),
        scratch_shapes=[
            pltpu.VMEM((tile + 2 * HALO, FF_CHUNK), F32),
            pltpu.VMEM((tile + 2 * HALO, FF_CHUNK), F32),
            pltpu.VMEM((tile, D_FF), BF16),
        ],
        compiler_params=_params(2),
        name="ffn",
    )(h, n2, n2, n2_meta, n2, wg, wu, cw, wd, g_final)


def _pair_layout(ang):
    c = jnp.cos(ang)
    s = jnp.sin(ang)
    return jnp.tile(c, (1, 4)), jnp.concatenate([-s, -s, s, s], axis=-1)


def _linear_inv():
    return ROPE_THETA ** (-jnp.arange(0, HEAD_DIM, 2, dtype=F32) / HEAD_DIM)


def _real_tables(s):
    t = jnp.arange(s)
    rowp = (t // GRID_W).astype(F32)
    colp = (t % GRID_W).astype(F32)
    axis_dim = HEAD_DIM // 2
    inv_a = ROPE_THETA ** (-jnp.arange(0, axis_dim, 2, dtype=F32) / axis_dim)
    ang_a = jnp.concatenate([rowp[:, None] * inv_a[None], colp[:, None] * inv_a[None]], axis=-1)
    pos = jnp.arange(N_META + s, dtype=F32)[N_META:]
    ang_b = pos[:, None] * _linear_inv()[None]
    return _pair_layout(ang_a) + _pair_layout(ang_b)


def _meta_tables():
    ang_a = jnp.zeros((META_ROWS, HEAD_DIM // 2), F32)
    pos = jnp.arange(META_ROWS, dtype=F32)
    ang_b = pos[:, None] * _linear_inv()[None]
    return _pair_layout(ang_a) + _pair_layout(ang_b)


def _attn_step_plan(s, tq):
    kv_bytes = N_KV * s * LANES * 2
    per_tile = tq * D_MODEL * (2 * 2 * 4 + 2 * 2 * 2 + 2)

    def tiles(vt_buffers):
        fixed = 2 * s * 2 * tq * 4 + (2 + vt_buffers) * kv_bytes + ATTN_TEMP_BYTES
        n = 1
        while 2 * n * tq <= s and fixed + 2 * n * per_tile <= VMEM_LIMIT:
            n *= 2
        return n

    return (tiles(1), 1) if tiles(1) > tiles(2) else (tiles(2), 2)


def _trunk(x, meta_x, meta_qkv, prm, cfg):
    s = x.shape[1]
    q, k, vt = _proj_call(x, _real_tables(s), prm["w_in"], prm["g_mix"], prm["gq"], prm["gk"],
                          prm["bd"], min(s, cfg["proj_tile"]))
    q_m, k_m, vt_m = meta_qkv
    kc = min(s, cfg["key_chunk"])
    tq = min(s, cfg["tq"])
    n_tiles, vt_buffers = _attn_step_plan(s, tq)
    h1, n2 = _attn_call(x, q, k, vt, k_m, vt_m, prm["lam"], prm["g_subln"], prm["w_out"],
                        prm["g_ffn"], tq, n_tiles, vt_buffers, kc, shared_q=False)
    n2_m = _meta_attn_call(meta_x[:, :N_META], q_m[:, :, :N_META], k, vt, k_m, vt_m, prm["lam"],
                           prm["g_subln"], prm["w_out"], prm["g_ffn"], kc)
    return _ffn_call(h1, n2, n2_m, prm["wg"], prm["wu"], prm["cw"], prm["wd"], prm["g_final"],
                     min(s, cfg["row_tile"]))


_CFG = dict(proj_tile=1024, tq=256, key_chunk=512, row_tile=1024)


def kernel(x_prompt, x_sample, meta_tokens, g_mix, w_in, g_qnorm_a, g_knorm_a, lambda_q1, lambda_k1,
           lambda_q2, lambda_k2, g_subln, w_out, g_ffn, w_ff_gate, w_ff_up, conv_w, conv_b,
           w_ff_down, g_final):
    assert w_in.shape[0] == 1, "single-layer trunk"
    d = D_MODEL
    head_order = np.array([0, 4, 1, 5, 2, 6, 3, 7])
    a_cols = (head_order[:, None] * HEAD_DIM + np.arange(HEAD_DIM)[None]).reshape(-1)
    in_perm = np.concatenate([a_cols, np.arange(A_PAIRS * LANES, IN_COLS)])
    out_perm = np.concatenate([a_cols, np.arange(A_PAIRS * LANES, d)])
    half = HEAD_DIM // 2
    tile_perm = np.concatenate([np.arange(0, half), np.arange(2 * half, 3 * half),
                                np.arange(half, 2 * half), np.arange(3 * half, 4 * half)])
    qk_tiles = list(range(0, 5)) + list(range(6, 14))
    for tile in qk_tiles:
        in_perm[tile * LANES:(tile + 1) * LANES] = in_perm[tile * LANES + tile_perm]
    lane_dim = np.concatenate([np.arange(half), np.arange(half),
                               np.arange(half, 2 * half), np.arange(half, 2 * half)])
    blk = (np.arange(LANES) // half) % 2
    cw = jnp.concatenate([conv_w[0], conv_b[0][None], jnp.zeros((4, D_FF), F32)], axis=0)
    lam = jnp.stack([lambda_q1[0], lambda_k1[0], lambda_q2[0], lambda_k2[0]])
    prm = dict(
        w_in=w_in[0][:, in_perm].astype(BF16),
        g_mix=g_mix[0][None],
        gq=g_qnorm_a[0][lane_dim][None],
        gk=g_knorm_a[0][lane_dim][None],
        bd=jnp.asarray((blk[:, None] == blk[None]) / HEAD_DIM, BF16),
        lam=jnp.zeros((8, LANES), F32).at[:4, :HEAD_DIM].set(lam),
        g_subln=g_subln[0][None],
        w_out=w_out[0][out_perm].astype(BF16),
        g_ffn=g_ffn[0][None],
        wg=w_ff_gate[0].astype(BF16),
        wu=w_ff_up[0].astype(BF16),
        cw=cw,
        wd=w_ff_down[0].astype(BF16),
        g_final=g_final[None],
    )
    meta_x = jnp.zeros((1, META_ROWS, d), F32).at[0, :N_META].set(meta_tokens)
    meta_qkv = _proj_call(meta_x, _meta_tables(), prm["w_in"], prm["g_mix"], prm["gq"], prm["gk"],
                          prm["bd"], META_ROWS)
    y_prompt = _trunk(x_prompt, meta_x, meta_qkv, prm, _CFG)
    y_sample = _trunk(x_sample, meta_x, meta_qkv, prm, _CFG)
    return (y_prompt, y_sample)
```

```python
import functools
import math

import jax
import jax.numpy as jnp
import numpy as np
from jax import lax
from jax.experimental import pallas as pl
from jax.experimental.pallas import tpu as pltpu

F32 = jnp.float32
BF16 = jnp.bfloat16

D_MODEL = 1024
HEAD_DIM = 64
N_META = 16
GRID_W = 64
ROPE_THETA = 10000.0
NORM_EPS = 1e-6
LANES = 128
N_PAIRS = 8
N_KV = 5
A_PAIRS = 4
D_FF = 2816
FF_CHUNK = 256
N_FF_CHUNKS = D_FF // FF_CHUNK
IN_COLS = 2304
HALO = 16
META_ROWS = 128
LAMBDA_INIT = 0.8 - 0.6 * math.exp(-0.3 * 0)
Q_SCALE = (HEAD_DIM ** -0.5) * math.log2(math.e)
VMEM_LIMIT = 56 * 1024 * 1024
ATTN_TEMP_BYTES = 11 * 1024 * 1024


def _params(n_grid_axes):
    return pltpu.CompilerParams(
        dimension_semantics=("arbitrary",) * n_grid_axes,
        vmem_limit_bytes=VMEM_LIMIT,
    )


def _const_spec(shape):
    zeros = (0,) * len(shape)
    return pl.BlockSpec(shape, lambda *_: zeros, pipeline_mode=pl.Buffered(1))


def _proj_kernel(x_ref, gmix_ref, w_ref, gq_ref, gk_ref, bd_ref, ca_ref, sa_ref, cb_ref, sb_ref,
                 q_ref, k_ref, vt_ref):
    x = x_ref[0]
    t = x.shape[0]
    ms = jnp.mean(x * x, axis=-1, keepdims=True)
    n = (x * lax.rsqrt(ms + NORM_EPS) * gmix_ref[...]).astype(BF16)

    def rope(y, c, s):
        return y * c + pltpu.roll(y, LANES // 2, 1) * s

    def proj(col):
        y = jnp.dot(n, w_ref[:, col:col + 2 * LANES], preferred_element_type=F32)
        return y[:, :LANES], y[:, LANES:]

    ca, sa, cb, sb = ca_ref[...], sa_ref[...], cb_ref[...], sb_ref[...]
    gq, gk = gq_ref[...], gk_ref[...]

    ya = list(proj(0) + proj(2 * LANES))
    yk, yv = proj(4 * LANES)
    vt_ref[0, 0] = yv.T.astype(BF16)
    sq = jnp.concatenate([(y * y).astype(BF16) for y in ya + [yk]], axis=0)
    ms_all = jnp.dot(sq, bd_ref[...], preferred_element_type=F32)
    for p in range(A_PAIRS):
        r = lax.rsqrt(ms_all[p * t:(p + 1) * t] + NORM_EPS) * Q_SCALE
        q_ref[0, p] = (rope(ya[p] * gq, ca, sa) * r).astype(BF16)
    r = lax.rsqrt(ms_all[A_PAIRS * t:] + NORM_EPS)
    k_ref[0, 0] = (rope(yk * gk, ca, sa) * r).astype(BF16)
    for j in range(2):
        y0, y1 = proj(14 * LANES + 2 * LANES * j)
        vt_ref[0, 1 + 2 * j] = y0.T.astype(BF16)
        vt_ref[0, 2 + 2 * j] = y1.T.astype(BF16)
    for j in range(2):
        y0, y1 = proj(10 * LANES + 2 * LANES * j)
        k_ref[0, 1 + 2 * j] = rope(y0, cb, sb).astype(BF16)
        k_ref[0, 2 + 2 * j] = rope(y1, cb, sb).astype(BF16)
    for j in range(2):
        y0, y1 = proj(6 * LANES + 2 * LANES * j)
        q_ref[0, A_PAIRS + 2 * j] = (rope(y0, cb, sb) * Q_SCALE).astype(BF16)
        q_ref[0, A_PAIRS + 2 * j + 1] = (rope(y1, cb, sb) * Q_SCALE).astype(BF16)


def _proj_call(x, tables, w_in, gmix, gq, gk, bd, tile):
    b, s, d = x.shape
    ca, sa, cb, sb = tables
    grid = (b, s // tile)
    tab_spec = pl.BlockSpec((tile, LANES), lambda bi, i: (i, 0))
    return pl.pallas_call(
        _proj_kernel,
        grid=grid,
        in_specs=[
            pl.BlockSpec((1, tile, d), lambda bi, i: (bi, i, 0)),
            _const_spec((1, d)),
            _const_spec((d, IN_COLS)),
            _const_spec((1, LANES)),
            _const_spec((1, LANES)),
            _const_spec((LANES, LANES)),
            tab_spec, tab_spec, tab_spec, tab_spec,
        ],
        out_specs=[
            pl.BlockSpec((1, N_PAIRS, tile, LANES), lambda bi, i: (bi, 0, i, 0)),
            pl.BlockSpec((1, N_KV, tile, LANES), lambda bi, i: (bi, 0, i, 0)),
            pl.BlockSpec((1, N_KV, LANES, tile), lambda bi, i: (bi, 0, 0, i)),
        ],
        out_shape=[
            jax.ShapeDtypeStruct((b, N_PAIRS, s, LANES), BF16),
            jax.ShapeDtypeStruct((b, N_KV, s, LANES), BF16),
            jax.ShapeDtypeStruct((b, N_KV, LANES, s), BF16),
        ],
        compiler_params=_params(2),
        name="proj",
    )(x, gmix, w_in, gq, gk, bd, ca, sa, cb, sb)


def _attn_kernel(x_ref, q_ref, k_ref, vt_ref, km_ref, vtm_ref, lam_ref, gsub_ref, wo_ref, gffn_ref,
                 h_ref, n_ref, sa_ref, sb_ref, acca_ref, accb_ref, la_ref, lb_ref, mix_ref,
                 *, tq, key_chunk):
    n_tiles = q_ref.shape[2] // tq
    s_len = k_ref.shape[2]
    n_chunks = s_len // key_chunk
    lane = lax.broadcasted_iota(jnp.int32, (tq, LANES), 1)
    q_low = (lane & (HEAD_DIM // 2)) == 0
    low = lane < HEAD_DIM
    contract_lanes = (((1,), (1,)), ((), ()))

    lam_p = lam_ref[...]
    lam = (jnp.exp(jnp.sum(lam_p[0:1] * lam_p[1:2], axis=-1, keepdims=True))
           - jnp.exp(jnp.sum(lam_p[2:3] * lam_p[3:4], axis=-1, keepdims=True))
           + LAMBDA_INIT)

    def chunk(c):
        return slice(c * key_chunk, (c + 1) * key_chunk)

    def locate(is_b, i):
        if isinstance(i, int):
            t, j = divmod(i, A_PAIRS)
        else:
            t, j = lax.shift_right_logical(i, 2), lax.bitwise_and(i, A_PAIRS - 1)
        return t, (j + A_PAIRS if is_b else j), (j + 1 if is_b else 0)

    def stage(scores_job=None, softmax_job=None, finish_job=None):
        if scores_job is not None:
            b1, i1, buf1 = scores_job
            t1, p1, kv1 = locate(b1, i1)
            row0 = t1 * tq if isinstance(t1, int) else pl.multiple_of(t1 * tq, tq)
            qp = q_ref[0, p1, pl.ds(row0, tq), :].astype(F32)
            qq = jnp.concatenate([jnp.where(q_low, qp, 0.0), jnp.where(q_low, 0.0, qp)],
                                 axis=0).astype(BF16)
            sm1 = lax.dot_general(km_ref[0, kv1][:N_META], qq, contract_lanes,
                                  preferred_element_type=F32)
            m8 = jnp.maximum(sm1[:8], sm1[8:])
        if softmax_job is not None:
            b2, i2, buf2, (sm2, m2) = softmax_job
            _, _, kv2 = locate(b2, i2)
            pm = jnp.exp2(sm2 - m2)
            l8 = pm[:8] + pm[8:]
            pm_pad = jnp.concatenate(
                [pm.astype(BF16), jnp.zeros((META_ROWS - N_META, 2 * tq), BF16)], axis=0)
            acc = jnp.dot(vtm_ref[0, kv2], pm_pad, preferred_element_type=F32)
        out = None
        if finish_job is not None:
            b3, i3 = finish_job
            t3, p3, _ = locate(b3, i3)
            acc_ref, l_ref = (accb_ref, lb_ref) if b3 else (acca_ref, la_ref)
            inv_l = 1.0 / jnp.sum(l_ref[...], axis=0, keepdims=True)
            ot = (acc_ref[...] * inv_l).T
            o_lo, o_hi = ot[:tq], ot[tq:]
            if b3:
                dt = o_lo - lam * o_hi
                msd = jnp.mean(dt * dt, axis=-1, keepdims=True)
                out = (dt * lax.rsqrt(msd + NORM_EPS) * gsub_ref[...]
                       * (1.0 - LAMBDA_INIT)).astype(BF16)
            else:
                out = jnp.where(low, o_lo, o_hi).astype(BF16)
            mix_ref[t3, p3] = out
        for c in range(n_chunks):
            if scores_job is not None:
                sc = lax.dot_general(k_ref[0, kv1, chunk(c), :], qq, contract_lanes,
                                     preferred_element_type=F32)
                buf1[chunk(c), :] = sc
                m8 = jnp.maximum(m8, jnp.max(sc.reshape(key_chunk // 8, 8, 2 * tq), axis=0))
            if softmax_job is not None:
                pc = jnp.exp2(buf2[chunk(c), :] - m2)
                l8 = l8 + jnp.sum(pc.reshape(key_chunk // 8, 8, 2 * tq), axis=0)
                acc = acc + jnp.dot(vt_ref[0, kv2, :, chunk(c)], pc.astype(BF16),
                                    preferred_element_type=F32)
        if softmax_job is not None:
            acc_ref, l_ref = (accb_ref, lb_ref) if b2 else (acca_ref, la_ref)
            acc_ref[...] = acc
            l_ref[...] = l8
        if scores_job is not None:
            return sm1, jnp.max(m8, axis=0, keepdims=True)
        return out

    grp_a, grp_b = False, True
    last = n_tiles * A_PAIRS - 1
    stats = stage(scores_job=(grp_a, 0, sa_ref))
    stats = stage(scores_job=(grp_b, 0, sb_ref), softmax_job=(grp_a, 0, sa_ref, stats))
    stats = stage(scores_job=(grp_a, 1, sa_ref), softmax_job=(grp_b, 0, sb_ref, stats),
                  finish_job=(grp_a, 0))

    def body(i, stats_a):
        stats_b = stage(scores_job=(grp_b, i, sb_ref), softmax_job=(grp_a, i, sa_ref, stats_a),
                        finish_job=(grp_b, i - 1))
        return stage(scores_job=(grp_a, i + 1, sa_ref), softmax_job=(grp_b, i, sb_ref, stats_b),
                     finish_job=(grp_a, i))

    stats = lax.fori_loop(0, (last - 1) // 2,
                          lambda k, st: body(2 * k + 2, body(2 * k + 1, st)), stats)
    stats = stage(scores_job=(grp_b, last, sb_ref), softmax_job=(grp_a, last, sa_ref, stats),
                  finish_job=(grp_b, last - 1))
    stage(softmax_job=(grp_b, last, sb_ref, stats), finish_job=(grp_a, last))
    out_last = stage(finish_job=(grp_b, last))

    split = (N_PAIRS - 1) * LANES
    for t in range(n_tiles):
        rows = slice(t * tq, (t + 1) * tq)
        if t + 1 < n_tiles:
            mix = jnp.concatenate([mix_ref[t, p] for p in range(N_PAIRS)], axis=1)
            h = x_ref[0, rows, :] + jnp.dot(mix, wo_ref[...], preferred_element_type=F32)
        else:
            mix = jnp.concatenate([mix_ref[t, p] for p in range(N_PAIRS - 1)], axis=1)
            h = (x_ref[0, rows, :]
                 + jnp.dot(mix, wo_ref[:split, :], preferred_element_type=F32)
                 + jnp.dot(out_last, wo_ref[split:, :], preferred_element_type=F32))
        h_ref[0, rows, :] = h
        ms = jnp.mean(h * h, axis=-1, keepdims=True)
        n_ref[0, rows, :] = (h * lax.rsqrt(ms + NORM_EPS) * gffn_ref[...]).astype(BF16)


def _attn_call(x, q, k, vt, km, vtm, lam_p, gsub, w_out, g_ffn, tq, n_tiles, vt_buffers, key_chunk):
    b, _, s, _ = k.shape
    d = x.shape[-1]
    rows = tq * n_tiles
    grid = (b, s // rows)
    return pl.pallas_call(
        functools.partial(_attn_kernel, tq=tq, key_chunk=key_chunk),
        grid=grid,
        in_specs=[
            pl.BlockSpec((1, rows, d), lambda bi, i: (bi, i, 0)),
            pl.BlockSpec((1, N_PAIRS, rows, LANES), lambda bi, i: (bi, 0, i, 0)),
            pl.BlockSpec((1, N_KV, s, LANES), lambda bi, i: (bi, 0, 0, 0)),
            pl.BlockSpec((1, N_KV, LANES, s), lambda bi, i: (bi, 0, 0, 0),
                         pipeline_mode=pl.Buffered(vt_buffers)),
            _const_spec((1, N_KV, META_ROWS, LANES)),
            _const_spec((1, N_KV, LANES, META_ROWS)),
            _const_spec((8, LANES)),
            _const_spec((1, LANES)),
            _const_spec((d, d)),
            _const_spec((1, d)),
        ],
        out_specs=[
            pl.BlockSpec((1, rows, d), lambda bi, i: (bi, i, 0)),
            pl.BlockSpec((1, rows, d), lambda bi, i: (bi, i, 0)),
        ],
        out_shape=[
            jax.ShapeDtypeStruct((b, s, d), F32),
            jax.ShapeDtypeStruct((b, s, d), BF16),
        ],
        scratch_shapes=[
            pltpu.VMEM((s, 2 * tq), F32),
            pltpu.VMEM((s, 2 * tq), F32),
            pltpu.VMEM((LANES, 2 * tq), F32),
            pltpu.VMEM((LANES, 2 * tq), F32),
            pltpu.VMEM((8, 2 * tq), F32),
            pltpu.VMEM((8, 2 * tq), F32),
            pltpu.VMEM((n_tiles, N_PAIRS, tq, LANES), BF16),
        ],
        compiler_params=_params(2),
        name="attn",
    )(x, q, k, vt, km, vtm, lam_p, gsub, w_out, g_ffn)


def _meta_attn_kernel(x_ref, q_ref, k_ref, vt_ref, km_ref, vtm_ref, lam_ref, gsub_ref, wo_ref,
                      gffn_ref, n_ref, s0_ref, s1_ref, *, key_chunk):
    s_len = k_ref.shape[2]
    n_chunks = s_len // key_chunk
    lane = lax.broadcasted_iota(jnp.int32, (N_META, LANES), 1)
    q_low = (lane & (HEAD_DIM // 2)) == 0
    low = lane < HEAD_DIM
    contract_lanes = (((1,), (1,)), ((), ()))

    lam_p = lam_ref[...]
    lam = (jnp.exp(jnp.sum(lam_p[0:1] * lam_p[1:2], axis=-1, keepdims=True))
           - jnp.exp(jnp.sum(lam_p[2:3] * lam_p[3:4], axis=-1, keepdims=True))
           + LAMBDA_INIT)

    def maps(p):
        qp = q_ref[0, p].astype(F32)
        return [jnp.where(q_low, qp, 0.0), jnp.where(q_low, 0.0, qp)]

    def attend(q_rows, kv, s_ref):
        qq = q_rows.astype(BF16)
        sm = lax.dot_general(km_ref[0, kv][:N_META], qq, contract_lanes,
                             preferred_element_type=F32)
        m8 = jnp.maximum(sm[:8], sm[8:])
        for c in range(n_chunks):
            rows = slice(c * key_chunk, (c + 1) * key_chunk)
            sc = lax.dot_general(k_ref[0, kv, rows, :], qq, contract_lanes,
                                 preferred_element_type=F32)
            s_ref[rows, :] = sc
            m8 = jnp.maximum(m8, jnp.max(sc.reshape(key_chunk // 8, 8, LANES), axis=0))
        m = jnp.max(m8, axis=0, keepdims=True)
        pm = jnp.exp2(sm - m)
        l8 = pm[:8] + pm[8:]
        pm_pad = jnp.concatenate(
            [pm.astype(BF16), jnp.zeros((META_ROWS - N_META, LANES), BF16)], axis=0)
        acc = jnp.dot(vtm_ref[0, kv], pm_pad, preferred_element_type=F32)
        for c in range(n_chunks):
            rows = slice(c * key_chunk, (c + 1) * key_chunk)
            pc = jnp.exp2(s_ref[rows, :] - m)
            l8 = l8 + jnp.sum(pc.reshape(key_chunk // 8, 8, LANES), axis=0)
            acc = acc + jnp.dot(vt_ref[0, kv, :, rows], pc.astype(BF16),
                                preferred_element_type=F32)
        return (acc * (1.0 / jnp.sum(l8, axis=0, keepdims=True))).T

    pieces = []
    ot = attend(jnp.concatenate([mp for p in range(A_PAIRS) for mp in maps(p)], axis=0), 0, s0_ref)
    for p in range(A_PAIRS):
        o_lo = ot[2 * N_META * p:2 * N_META * p + N_META]
        o_hi = ot[2 * N_META * p + N_META:2 * N_META * (p + 1)]
        pieces.append(jnp.where(low, o_lo, o_hi).astype(BF16))
    idle = jnp.zeros((LANES - 2 * N_META, LANES), F32)
    for hd in range(N_PAIRS - A_PAIRS):
        ot = attend(jnp.concatenate(maps(A_PAIRS + hd) + [idle], axis=0), hd + 1,
                    s1_ref if hd % 2 == 0 else s0_ref)
        dt = ot[:N_META] - lam * ot[N_META:2 * N_META]
        msd = jnp.mean(dt * dt, axis=-1, keepdims=True)
        pieces.append((dt * lax.rsqrt(msd + NORM_EPS) * gsub_ref[...]
                       * (1.0 - LAMBDA_INIT)).astype(BF16))
    mix = jnp.concatenate(pieces, axis=1)
    h = x_ref[0] + jnp.dot(mix, wo_ref[...], preferred_element_type=F32)
    ms = jnp.mean(h * h, axis=-1, keepdims=True)
    n_ref[0] = (h * lax.rsqrt(ms + NORM_EPS) * gffn_ref[...]).astype(BF16)


def _meta_attn_call(x_meta, q_meta, k, vt, km, vtm, lam_p, gsub, w_out, g_ffn, key_chunk):
    b, _, s, _ = k.shape
    d = x_meta.shape[-1]
    return pl.pallas_call(
        functools.partial(_meta_attn_kernel, key_chunk=key_chunk),
        grid=(b,),
        in_specs=[
            _const_spec((1, N_META, d)),
            _const_spec((1, N_PAIRS, N_META, LANES)),
            pl.BlockSpec((1, N_KV, s, LANES), lambda bi: (bi, 0, 0, 0)),
            pl.BlockSpec((1, N_KV, LANES, s), lambda bi: (bi, 0, 0, 0)),
            _const_spec((1, N_KV, META_ROWS, LANES)),
            _const_spec((1, N_KV, LANES, META_ROWS)),
            _const_spec((8, LANES)),
            _const_spec((1, LANES)),
            _const_spec((d, d)),
            _const_spec((1, d)),
        ],
        out_specs=pl.BlockSpec((1, N_META, d), lambda bi: (bi, 0, 0)),
        out_shape=jax.ShapeDtypeStruct((b, N_META, d), BF16),
        scratch_shapes=[pltpu.VMEM((s, LANES), F32), pltpu.VMEM((s, LANES), F32)],
        compiler_params=_params(1),
        name="meta_attn",
    )(x_meta, q_meta, k, vt, km, vtm, lam_p, gsub, w_out, g_ffn)


def _ffn_kernel(h_ref, n_ref, left_ref, meta_ref, right_ref, wg_ref, wu_ref, cw_ref, wd_ref,
                gfin_ref, o_ref, gate0_ref, gate1_ref, u_ref):
    i = pl.program_id(1)
    last = pl.num_programs(1) - 1
    t = n_ref.shape[1]
    n = n_ref[0]
    left = jnp.where(i == 0, meta_ref[0], left_ref[0])
    right = jnp.where(i == last, jnp.zeros_like(right_ref[0]), right_ref[0])
    n_ext = jnp.concatenate([left, n, right], axis=0)

    for c in range(N_FF_CHUNKS):
        gate_ref = gate1_ref if c % 2 else gate0_ref
        cols = slice(c * FF_CHUNK, (c + 1) * FF_CHUNK)
        gate_ref[...] = jnp.dot(n_ext, wg_ref[:, cols], preferred_element_type=F32)
        cw = cw_ref[:, cols]
        g = (gate_ref[HALO - 1:HALO - 1 + t, :] * cw[0:1]
             + gate_ref[HALO:HALO + t, :] * cw[1:2]
             + gate_ref[HALO + 1:HALO + 1 + t, :] * cw[2:3]
             + cw[3:4])
        up = jnp.dot(n, wu_ref[:, cols], preferred_element_type=F32)
        act = 0.5 * g * (1.0 + lax.erf(g * (2.0 ** -0.5)))
        u_ref[:, cols] = (act * up).astype(BF16)
    y = h_ref[0] + jnp.dot(u_ref[...], wd_ref[...], preferred_element_type=F32)
    ms = jnp.mean(y * y, axis=-1, keepdims=True)
    o_ref[0] = y * lax.rsqrt(ms + NORM_EPS) * gfin_ref[...]


def _ffn_call(h, n2, n2_meta, wg, wu, cw, wd, g_final, tile):
    b, s, d = h.shape
    grid = (b, s // tile)
    per = tile // HALO
    n_halo_blocks = s // HALO
    return pl.pallas_call(
        _ffn_kernel,
        grid=grid,
        in_specs=[
            pl.BlockSpec((1, tile, d), lambda bi, i: (bi, i, 0)),
            pl.BlockSpec((1, tile, d), lambda bi, i: (bi, i, 0)),
            pl.BlockSpec((1, HALO, d), lambda bi, i: (bi, jnp.maximum(i * per - 1, 0), 0)),
            pl.BlockSpec((1, HALO, d), lambda bi, i: (bi, 0, 0)),
            pl.BlockSpec((1, HALO, d),
                         lambda bi, i: (bi, jnp.minimum((i + 1) * per, n_halo_blocks - 1), 0)),
            _const_spec((d, D_FF)),
            _const_spec((d, D_FF)),
            _const_spec((8, D_FF)),
            _const_spec((D_FF, d)),
            _const_spec((1, d)),
        ],
        out_specs=pl.BlockSpec((1, tile, d), lambda bi, i: (bi, i, 0)),
        out_shape=jax.ShapeDtypeStruct((b, s, d), F32),
        scratch_shapes=[
            pltpu.VMEM((tile + 2 * HALO, FF_CHUNK), F32),
            pltpu.VMEM((tile + 2 * HALO, FF_CHUNK), F32),
            pltpu.VMEM((tile, D_FF), BF16),
        ],
        compiler_params=_params(2),
        name="ffn",
    )(h, n2, n2, n2_meta, n2, wg, wu, cw, wd, g_final)


def _pair_layout(ang):
    c = jnp.cos(ang)
    s = jnp.sin(ang)
    return jnp.tile(c, (1, 4)), jnp.concatenate([-s, -s, s, s], axis=-1)


def _linear_inv():
    return ROPE_THETA ** (-jnp.arange(0, HEAD_DIM, 2, dtype=F32) / HEAD_DIM)


def _real_tables(s):
    t = jnp.arange(s)
    rowp = (t // GRID_W).astype(F32)
    colp = (t % GRID_W).astype(F32)
    axis_dim = HEAD_DIM // 2
    inv_a = ROPE_THETA ** (-jnp.arange(0, axis_dim, 2, dtype=F32) / axis_dim)
    ang_a = jnp.concatenate([rowp[:, None] * inv_a[None], colp[:, None] * inv_a[None]], axis=-1)
    pos = jnp.arange(N_META + s, dtype=F32)[N_META:]
    ang_b = pos[:, None] * _linear_inv()[None]
    return _pair_layout(ang_a) + _pair_layout(ang_b)


def _meta_tables():
    ang_a = jnp.zeros((META_ROWS, HEAD_DIM // 2), F32)
    pos = jnp.arange(META_ROWS, dtype=F32)
    ang_b = pos[:, None] * _linear_inv()[None]
    return _pair_layout(ang_a) + _pair_layout(ang_b)


def _attn_step_plan(s, tq):
    kv_bytes = N_KV * s * LANES * 2
    per_tile = tq * D_MODEL * (2 * 2 * 4 + 2 * 2 * 2 + 2)

    def tiles(vt_buffers):
        fixed = 2 * s * 2 * tq * 4 + (2 + vt_buffers) * kv_bytes + ATTN_TEMP_BYTES
        n = 1
        while 2 * n * tq <= s and fixed + 2 * n * per_tile <= VMEM_LIMIT:
            n *= 2
        return n

    return (tiles(1), 1) if tiles(1) > tiles(2) else (tiles(2), 2)


def _trunk(x, meta_x, meta_qkv, prm, cfg):
    s = x.shape[1]
    q, k, vt = _proj_call(x, _real_tables(s), prm["w_in"], prm["g_mix"], prm["gq"], prm["gk"],
                          prm["bd"], min(s, cfg["proj_tile"]))
    q_m, k_m, vt_m = meta_qkv
    kc = min(s, cfg["key_chunk"])
    tq = min(s, cfg["tq"])
    n_tiles, vt_buffers = _attn_step_plan(s, tq)
    h1, n2 = _attn_call(x, q, k, vt, k_m, vt_m, prm["lam"], prm["g_subln"], prm["w_out"],
                        prm["g_ffn"], tq, n_tiles, vt_buffers, kc)
    n2_m = _meta_attn_call(meta_x[:, :N_META], q_m[:, :, :N_META], k, vt, k_m, vt_m, prm["lam"],
                           prm["g_subln"], prm["w_out"], prm["g_ffn"], kc)
    return _ffn_call(h1, n2, n2_m, prm["wg"], prm["wu"], prm["cw"], prm["wd"], prm["g_final"],
                     min(s, cfg["row_tile"]))


_CFG = dict(proj_tile=1024, tq=256, key_chunk=512, row_tile=1024)


def kernel(x_prompt, x_sample, meta_tokens, g_mix, w_in, g_qnorm_a, g_knorm_a, lambda_q1, lambda_k1,
           lambda_q2, lambda_k2, g_subln, w_out, g_ffn, w_ff_gate, w_ff_up, conv_w, conv_b,
           w_ff_down, g_final):
    assert w_in.shape[0] == 1, "single-layer trunk"
    d = D_MODEL
    head_order = np.array([0, 4, 1, 5, 2, 6, 3, 7])
    a_cols = (head_order[:, None] * HEAD_DIM + np.arange(HEAD_DIM)[None]).reshape(-1)
    in_perm = np.concatenate([a_cols, np.arange(A_PAIRS * LANES, IN_COLS)])
    out_perm = np.concatenate([a_cols, np.arange(A_PAIRS * LANES, d)])
    half = HEAD_DIM // 2
    tile_perm = np.concatenate([np.arange(0, half), np.arange(2 * half, 3 * half),
                                np.arange(half, 2 * half), np.arange(3 * half, 4 * half)])
    qk_tiles = list(range(0, 5)) + list(range(6, 14))
    for tile in qk_tiles:
        in_perm[tile * LANES:(tile + 1) * LANES] = in_perm[tile * LANES + tile_perm]
    lane_dim = np.concatenate([np.arange(half), np.arange(half),
                               np.arange(half, 2 * half), np.arange(half, 2 * half)])
    blk = (np.arange(LANES) // half) % 2
    cw = jnp.concatenate([conv_w[0], conv_b[0][None], jnp.zeros((4, D_FF), F32)], axis=0)
    lam = jnp.stack([lambda_q1[0], lambda_k1[0], lambda_q2[0], lambda_k2[0]])
    prm = dict(
        w_in=w_in[0][:, in_perm].astype(BF16),
        g_mix=g_mix[0][None],
        gq=g_qnorm_a[0][lane_dim][None],
        gk=g_knorm_a[0][lane_dim][None],
        bd=jnp.asarray((blk[:, None] == blk[None]) / HEAD_DIM, BF16),
        lam=jnp.zeros((8, LANES), F32).at[:4, :HEAD_DIM].set(lam),
        g_subln=g_subln[0][None],
        w_out=w_out[0][out_perm].astype(BF16),
        g_ffn=g_ffn[0][None],
        wg=w_ff_gate[0].astype(BF16),
        wu=w_ff_up[0].astype(BF16),
        cw=cw,
        wd=w_ff_down[0].astype(BF16),
        g_final=g_final[None],
    )
    meta_x = jnp.zeros((1, META_ROWS, d), F32).at[0, :N_META].set(meta_tokens)
    meta_qkv = _proj_call(meta_x, _meta_tables(), prm["w_in"], prm["g_mix"], prm["gq"], prm["gk"],
                          prm["bd"], META_ROWS)
    y_prompt = _trunk(x_prompt, meta_x, meta_qkv, prm, _CFG)
    y_sample = _trunk(x_sample, meta_x, meta_qkv, prm, _CFG)
    return (y_prompt, y_sample)
```

```python
import functools
import math

import jax
import jax.numpy as jnp
import numpy as np
from jax import lax
from jax.experimental import pallas as pl
from jax.experimental.pallas import tpu as pltpu

F32 = jnp.float32
BF16 = jnp.bfloat16

D_MODEL = 1024
HEAD_DIM = 64
N_META = 16
GRID_W = 64
ROPE_THETA = 10000.0
NORM_EPS = 1e-6
LANES = 128
N_PAIRS = 8
N_KV = 5
A_PAIRS = 4
D_FF = 2816
FF_CHUNK = 256
N_FF_CHUNKS = D_FF // FF_CHUNK
IN_COLS = 2304
HALO = 16
META_ROWS = 128
LAMBDA_INIT = 0.8 - 0.6 * math.exp(-0.3 * 0)
Q_SCALE = (HEAD_DIM ** -0.5) * math.log2(math.e)
VMEM_LIMIT = 56 * 1024 * 1024
ATTN_TEMP_BYTES = 11 * 1024 * 1024


def _params(n_grid_axes):
    return pltpu.CompilerParams(
        dimension_semantics=("arbitrary",) * n_grid_axes,
        vmem_limit_bytes=VMEM_LIMIT,
    )


def _const_spec(shape):
    zeros = (0,) * len(shape)
    return pl.BlockSpec(shape, lambda *_: zeros, pipeline_mode=pl.Buffered(1))


def _proj_kernel(x_ref, gmix_ref, w_ref, gq_ref, gk_ref, bd_ref, ca_ref, sa_ref, cb_ref, sb_ref,
                 q_ref, k_ref, vt_ref):
    x = x_ref[0]
    t = x.shape[0]
    ms = jnp.mean(x * x, axis=-1, keepdims=True)
    n = (x * lax.rsqrt(ms + NORM_EPS) * gmix_ref[...]).astype(BF16)

    def rope(y, c, s):
        return y * c + pltpu.roll(y, LANES // 2, 1) * s

    def proj(col):
        y = jnp.dot(n, w_ref[:, col:col + 2 * LANES], preferred_element_type=F32)
        return y[:, :LANES], y[:, LANES:]

    ca, sa, cb, sb = ca_ref[...], sa_ref[...], cb_ref[...], sb_ref[...]
    gq, gk = gq_ref[...], gk_ref[...]

    ya = list(proj(0) + proj(2 * LANES))
    yk, yv = proj(4 * LANES)
    vt_ref[0, 0] = yv.T.astype(BF16)
    sq = jnp.concatenate([(y * y).astype(BF16) for y in ya + [yk]], axis=0)
    ms_all = jnp.dot(sq, bd_ref[...], preferred_element_type=F32)
    for p in range(A_PAIRS):
        r = lax.rsqrt(ms_all[p * t:(p + 1) * t] + NORM_EPS) * Q_SCALE
        q_ref[0, p] = (rope(ya[p] * gq, ca, sa) * r).astype(BF16)
    r = lax.rsqrt(ms_all[A_PAIRS * t:] + NORM_EPS)
    k_ref[0, 0] = (rope(yk * gk, ca, sa) * r).astype(BF16)
    for j in range(2):
        y0, y1 = proj(14 * LANES + 2 * LANES * j)
        vt_ref[0, 1 + 2 * j] = y0.T.astype(BF16)
        vt_ref[0, 2 + 2 * j] = y1.T.astype(BF16)
    for j in range(2):
        y0, y1 = proj(10 * LANES + 2 * LANES * j)
        k_ref[0, 1 + 2 * j] = rope(y0, cb, sb).astype(BF16)
        k_ref[0, 2 + 2 * j] = rope(y1, cb, sb).astype(BF16)
    for j in range(2):
        y0, y1 = proj(6 * LANES + 2 * LANES * j)
        q_ref[0, A_PAIRS + 2 * j] = (rope(y0, cb, sb) * Q_SCALE).astype(BF16)
        q_ref[0, A_PAIRS + 2 * j + 1] = (rope(y1, cb, sb) * Q_SCALE).astype(BF16)


def _proj_call(x, tables, w_in, gmix, gq, gk, bd, tile):
    b, s, d = x.shape
    ca, sa, cb, sb = tables
    grid = (b, s // tile)
    tab_spec = pl.BlockSpec((tile, LANES), lambda bi, i: (i, 0))
    return pl.pallas_call(
        _proj_kernel,
        grid=grid,
        in_specs=[
            pl.BlockSpec((1, tile, d), lambda bi, i: (bi, i, 0)),
            _const_spec((1, d)),
            _const_spec((d, IN_COLS)),
            _const_spec((1, LANES)),
            _const_spec((1, LANES)),
            _const_spec((LANES, LANES)),
            tab_spec, tab_spec, tab_spec, tab_spec,
        ],
        out_specs=[
            pl.BlockSpec((1, N_PAIRS, tile, LANES), lambda bi, i: (bi, 0, i, 0)),
            pl.BlockSpec((1, N_KV, tile, LANES), lambda bi, i: (bi, 0, i, 0)),
            pl.BlockSpec((1, N_KV, LANES, tile), lambda bi, i: (bi, 0, 0, i)),
        ],
        out_shape=[
            jax.ShapeDtypeStruct((b, N_PAIRS, s, LANES), BF16),
            jax.ShapeDtypeStruct((b, N_KV, s, LANES), BF16),
            jax.ShapeDtypeStruct((b, N_KV, LANES, s), BF16),
        ],
        compiler_params=_params(2),
        name="proj",
    )(x, gmix, w_in, gq, gk, bd, ca, sa, cb, sb)


def _attn_kernel(x_ref, q_ref, k_ref, vt_ref, km_ref, vtm_ref, lam_ref, gsub_ref, wo_ref, gffn_ref,
                 h_ref, n_ref, sa_ref, sb_ref, acca_ref, accb_ref, la_ref, lb_ref, mix_ref,
                 *, tq, key_chunk):
    n_tiles = q_ref.shape[2] // tq
    s_len = k_ref.shape[2]
    n_chunks = s_len // key_chunk
    lane = lax.broadcasted_iota(jnp.int32, (tq, LANES), 1)
    q_low = (lane & (HEAD_DIM // 2)) == 0
    low = lane < HEAD_DIM
    contract_lanes = (((1,), (1,)), ((), ()))

    lam_p = lam_ref[...]
    lam = (jnp.exp(jnp.sum(lam_p[0:1] * lam_p[1:2], axis=-1, keepdims=True))
           - jnp.exp(jnp.sum(lam_p[2:3] * lam_p[3:4], axis=-1, keepdims=True))
           + LAMBDA_INIT)

    def chunk(c):
        return slice(c * key_chunk, (c + 1) * key_chunk)

    def locate(is_b, i):
        if isinstance(i, int):
            t, j = divmod(i, A_PAIRS)
        else:
            t, j = lax.shift_right_logical(i, 2), lax.bitwise_and(i, A_PAIRS - 1)
        return t, (j + A_PAIRS if is_b else j), (j + 1 if is_b else 0)

    def stage(scores_job=None, softmax_job=None, finish_job=None):
        if scores_job is not None:
            b1, i1, buf1 = scores_job
            t1, p1, kv1 = locate(b1, i1)
            row0 = t1 * tq if isinstance(t1, int) else pl.multiple_of(t1 * tq, tq)
            qp = q_ref[0, p1, pl.ds(row0, tq), :].astype(F32)
            qq = jnp.concatenate([jnp.where(q_low, qp, 0.0), jnp.where(q_low, 0.0, qp)],
                                 axis=0).astype(BF16)
            sm1 = lax.dot_general(km_ref[0, kv1][:N_META], qq, contract_lanes,
                                  preferred_element_type=F32)
            m8 = jnp.maximum(sm1[:8], sm1[8:])
        if softmax_job is not None:
            b2, i2, buf2, (sm2, m2) = softmax_job
            _, _, kv2 = locate(b2, i2)
            pm = jnp.exp2(sm2 - m2)
            l8 = pm[:8] + pm[8:]
            pm_pad = jnp.concatenate(
                [pm.astype(BF16), jnp.zeros((META_ROWS - N_META, 2 * tq), BF16)], axis=0)
            acc = jnp.dot(vtm_ref[0, kv2], pm_pad, preferred_element_type=F32)
        out = None
        if finish_job is not None:
            b3, i3 = finish_job
            t3, p3, _ = locate(b3, i3)
            acc_ref, l_ref = (accb_ref, lb_ref) if b3 else (acca_ref, la_ref)
            inv_l = 1.0 / jnp.sum(l_ref[...], axis=0, keepdims=True)
            ot = (acc_ref[...] * inv_l).T
            o_lo, o_hi = ot[:tq], ot[tq:]
            if b3:
                dt = o_lo - lam * o_hi
                msd = jnp.mean(dt * dt, axis=-1, keepdims=True)
                out = (dt * lax.rsqrt(msd + NORM_EPS) * gsub_ref[...]
                       * (1.0 - LAMBDA_INIT)).astype(BF16)
            else:
                out = jnp.where(low, o_lo, o_hi).astype(BF16)
            mix_ref[t3, p3] = out
        for c in range(n_chunks):
            if scores_job is not None:
                sc = lax.dot_general(k_ref[0, kv1, chunk(c), :], qq, contract_lanes,
                                     preferred_element_type=F32)
                buf1[chunk(c), :] = sc
                m8 = jnp.maximum(m8, jnp.max(sc.reshape(key_chunk // 8, 8, 2 * tq), axis=0))
            if softmax_job is not None:
                pc = jnp.exp2(buf2[chunk(c), :] - m2)
                l8 = l8 + jnp.sum(pc.reshape(key_chunk // 8, 8, 2 * tq), axis=0)
                acc = acc + jnp.dot(vt_ref[0, kv2, :, chunk(c)], pc.astype(BF16),
                                    preferred_element_type=F32)
        if softmax_job is not None:
            acc_ref, l_ref = (accb_ref, lb_ref) if b2 else (acca_ref, la_ref)
            acc_ref[...] = acc
            l_ref[...] = l8
        if scores_job is not None:
            return sm1, jnp.max(m8, axis=0, keepdims=True)
        return out

    grp_a, grp_b = False, True
    last = n_tiles * A_PAIRS - 1
    stats = stage(scores_job=(grp_a, 0, sa_ref))
    stats = stage(scores_job=(grp_b, 0, sb_ref), softmax_job=(grp_a, 0, sa_ref, stats))
    stats = stage(scores_job=(grp_a, 1, sa_ref), softmax_job=(grp_b, 0, sb_ref, stats),
                  finish_job=(grp_a, 0))

    def body(i, stats_a):
        stats_b = stage(scores_job=(grp_b, i, sb_ref), softmax_job=(grp_a, i, sa_ref, stats_a),
                        finish_job=(grp_b, i - 1))
        return stage(scores_job=(grp_a, i + 1, sa_ref), softmax_job=(grp_b, i, sb_ref, stats_b),
                     finish_job=(grp_a, i))

    stats = lax.fori_loop(0, (last - 1) // 2,
                          lambda k, st: body(2 * k + 2, body(2 * k + 1, st)), stats)
    stats = stage(scores_job=(grp_b, last, sb_ref), softmax_job=(grp_a, last, sa_ref, stats),
                  finish_job=(grp_b, last - 1))
    stage(softmax_job=(grp_b, last, sb_ref, stats), finish_job=(grp_a, last))
    out_last = stage(finish_job=(grp_b, last))

    split = (N_PAIRS - 1) * LANES
    for t in range(n_tiles):
        rows = slice(t * tq, (t + 1) * tq)
        if t + 1 < n_tiles:
            mix = jnp.concatenate([mix_ref[t, p] for p in range(N_PAIRS)], axis=1)
            h = x_ref[0, rows, :] + jnp.dot(mix, wo_ref[...], preferred_element_type=F32)
        else:
            mix = jnp.concatenate([mix_ref[t, p] for p in range(N_PAIRS - 1)], axis=1)
            h = (x_ref[0, rows, :]
                 + jnp.dot(mix, wo_ref[:split, :], preferred_element_type=F32)
                 + jnp.dot(out_last, wo_ref[split:, :], preferred_element_type=F32))
        h_ref[0, rows, :] = h
        ms = jnp.mean(h * h, axis=-1, keepdims=True)
        n_ref[0, rows, :] = (h * lax.rsqrt(ms + NORM_EPS) * gffn_ref[...]).astype(BF16)


def _attn_call(x, q, k, vt, km, vtm, lam_p, gsub, w_out, g_ffn, tq, n_tiles, vt_buffers, key_chunk):
    b, _, s, _ = k.shape
    d = x.shape[-1]
    rows = tq * n_tiles
    grid = (b, s // rows)
    return pl.pallas_call(
        functools.partial(_attn_kernel, tq=tq, key_chunk=key_chunk),
        grid=grid,
        in_specs=[
            pl.BlockSpec((1, rows, d), lambda bi, i: (bi, i, 0)),
            pl.BlockSpec((1, N_PAIRS, rows, LANES), lambda bi, i: (bi, 0, i, 0)),
            pl.BlockSpec((1, N_KV, s, LANES), lambda bi, i: (bi, 0, 0, 0)),
            pl.BlockSpec((1, N_KV, LANES, s), lambda bi, i: (bi, 0, 0, 0),
                         pipeline_mode=pl.Buffered(vt_buffers)),
            _const_spec((1, N_KV, META_ROWS, LANES)),
            _const_spec((1, N_KV, LANES, META_ROWS)),
            _const_spec((8, LANES)),
            _const_spec((1, LANES)),
            _const_spec((d, d)),
            _const_spec((1, d)),
        ],
        out_specs=[
            pl.BlockSpec((1, rows, d), lambda bi, i: (bi, i, 0)),
            pl.BlockSpec((1, rows, d), lambda bi, i: (bi, i, 0)),
        ],
        out_shape=[
            jax.ShapeDtypeStruct((b, s, d), F32),
            jax.ShapeDtypeStruct((b, s, d), BF16),
        ],
        scratch_shapes=[
            pltpu.VMEM((s, 2 * tq), F32),
            pltpu.VMEM((s, 2 * tq), F32),
            pltpu.VMEM((LANES, 2 * tq), F32),
            pltpu.VMEM((LANES, 2 * tq), F32),
            pltpu.VMEM((8, 2 * tq), F32),
            pltpu.VMEM((8, 2 * tq), F32),
            pltpu.VMEM((n_tiles, N_PAIRS, tq, LANES), BF16),
        ],
        compiler_params=_params(2),
        name="attn",
    )(x, q, k, vt, km, vtm, lam_p, gsub, w_out, g_ffn)


def _meta_attn_kernel(x_ref, q_ref, k_ref, vt_ref, km_ref, vtm_ref, lam_ref, gsub_ref, wo_ref,
                      gffn_ref, n_ref, s0_ref, s1_ref, *, key_chunk):
    s_len = k_ref.shape[2]
    n_chunks = s_len // key_chunk
    lane = lax.broadcasted_iota(jnp.int32, (N_META, LANES), 1)
    q_low = (lane & (HEAD_DIM // 2)) == 0
    low = lane < HEAD_DIM
    contract_lanes = (((1,), (1,)), ((), ()))

    lam_p = lam_ref[...]
    lam = (jnp.exp(jnp.sum(lam_p[0:1] * lam_p[1:2], axis=-1, keepdims=True))
           - jnp.exp(jnp.sum(lam_p[2:3] * lam_p[3:4], axis=-1, keepdims=True))
           + LAMBDA_INIT)

    def maps(p):
        qp = q_ref[0, p].astype(F32)
        return [jnp.where(q_low, qp, 0.0), jnp.where(q_low, 0.0, qp)]

    def cat(parts, axis):
        return parts[0] if len(parts) == 1 else jnp.concatenate(parts, axis=axis)

    def attend(q_rows, kvs, s_ref):
        qq = q_rows.astype(BF16)
        sm = lax.dot_general(cat([km_ref[0, kv][:N_META] for kv in kvs], 1), qq, contract_lanes,
                             preferred_element_type=F32)
        m8 = jnp.maximum(sm[:8], sm[8:])
        for c in range(n_chunks):
            rows = slice(c * key_chunk, (c + 1) * key_chunk)
            sc = lax.dot_general(cat([k_ref[0, kv, rows, :] for kv in kvs], 1), qq,
                                 contract_lanes, preferred_element_type=F32)
            s_ref[rows, :] = sc
            m8 = jnp.maximum(m8, jnp.max(sc.reshape(key_chunk // 8, 8, LANES), axis=0))
        m = jnp.max(m8, axis=0, keepdims=True)
        pm = jnp.exp2(sm - m)
        l8 = pm[:8] + pm[8:]
        pm_pad = jnp.concatenate(
            [pm.astype(BF16), jnp.zeros((META_ROWS - N_META, LANES), BF16)], axis=0)
        acc = jnp.dot(cat([vtm_ref[0, kv] for kv in kvs], 0), pm_pad,
                      preferred_element_type=F32)
        for c in range(n_chunks):
            rows = slice(c * key_chunk, (c + 1) * key_chunk)
            pc = jnp.exp2(s_ref[rows, :] - m)
            l8 = l8 + jnp.sum(pc.reshape(key_chunk // 8, 8, LANES), axis=0)
            acc = acc + jnp.dot(cat([vt_ref[0, kv, :, rows] for kv in kvs], 0), pc.astype(BF16),
                                preferred_element_type=F32)
        inv_l = 1.0 / jnp.sum(l8, axis=0, keepdims=True)
        return [(acc[LANES * j:LANES * (j + 1)] * inv_l).T for j in range(len(kvs))]

    pieces = []
    ot, = attend(jnp.concatenate([mp for p in range(A_PAIRS) for mp in maps(p)], axis=0), (0,),
                 s0_ref)
    for p in range(A_PAIRS):
        o_lo = ot[2 * N_META * p:2 * N_META * p + N_META]
        o_hi = ot[2 * N_META * p + N_META:2 * N_META * (p + 1)]
        pieces.append(jnp.where(low, o_lo, o_hi).astype(BF16))
    zeros = jnp.zeros((2 * N_META, LANES), F32)
    idle = jnp.zeros((LANES - 4 * N_META, 2 * LANES), F32)
    for jb in range((N_PAIRS - A_PAIRS) // 2):
        hd = (2 * jb, 2 * jb + 1)
        first = jnp.concatenate([jnp.concatenate(maps(A_PAIRS + hd[0]), axis=0), zeros], axis=1)
        second = jnp.concatenate([zeros, jnp.concatenate(maps(A_PAIRS + hd[1]), axis=0)], axis=1)
        outs = attend(jnp.concatenate([first, second, idle], axis=0), (hd[0] + 1, hd[1] + 1),
                      s1_ref if jb % 2 == 0 else s0_ref)
        for j, ot in enumerate(outs):
            base = 2 * N_META * j
            dt = ot[base:base + N_META] - lam * ot[base + N_META:base + 2 * N_META]
            msd = jnp.mean(dt * dt, axis=-1, keepdims=True)
            pieces.append((dt * lax.rsqrt(msd + NORM_EPS) * gsub_ref[...]
                           * (1.0 - LAMBDA_INIT)).astype(BF16))
    mix = jnp.concatenate(pieces, axis=1)
    h = x_ref[0] + jnp.dot(mix, wo_ref[...], preferred_element_type=F32)
    ms = jnp.mean(h * h, axis=-1, keepdims=True)
    n_ref[0] = (h * lax.rsqrt(ms + NORM_EPS) * gffn_ref[...]).astype(BF16)


def _meta_attn_call(x_meta, q_meta, k, vt, km, vtm, lam_p, gsub, w_out, g_ffn, key_chunk):
    b, _, s, _ = k.shape
    d = x_meta.shape[-1]
    return pl.pallas_call(
        functools.partial(_meta_attn_kernel, key_chunk=key_chunk),
        grid=(b,),
        in_specs=[
            _const_spec((1, N_META, d)),
            _const_spec((1, N_PAIRS, N_META, LANES)),
            pl.BlockSpec((1, N_KV, s, LANES), lambda bi: (bi, 0, 0, 0)),
            pl.BlockSpec((1, N_KV, LANES, s), lambda bi: (bi, 0, 0, 0)),
            _const_spec((1, N_KV, META_ROWS, LANES)),
            _const_spec((1, N_KV, LANES, META_ROWS)),
            _const_spec((8, LANES)),
            _const_spec((1, LANES)),
            _const_spec((d, d)),
            _const_spec((1, d)),
        ],
        out_specs=pl.BlockSpec((1, N_META, d), lambda bi: (bi, 0, 0)),
        out_shape=jax.ShapeDtypeStruct((b, N_META, d), BF16),
        scratch_shapes=[pltpu.VMEM((s, LANES), F32), pltpu.VMEM((s, LANES), F32)],
        compiler_params=_params(1),
        name="meta_attn",
    )(x_meta, q_meta, k, vt, km, vtm, lam_p, gsub, w_out, g_ffn)


def _ffn_kernel(h_ref, n_ref, left_ref, meta_ref, right_ref, wg_ref, wu_ref, cw_ref, wd_ref,
                gfin_ref, o_ref, gate0_ref, gate1_ref, u_ref):
    i = pl.program_id(1)
    last = pl.num_programs(1) - 1
    t = n_ref.shape[1]
    n = n_ref[0]
    left = jnp.where(i == 0, meta_ref[0], left_ref[0])
    right = jnp.where(i == last, jnp.zeros_like(right_ref[0]), right_ref[0])
    n_ext = jnp.concatenate([left, n, right], axis=0)

    for c in range(N_FF_CHUNKS):
        gate_ref = gate1_ref if c % 2 else gate0_ref
        cols = slice(c * FF_CHUNK, (c + 1) * FF_CHUNK)
        gate_ref[...] = jnp.dot(n_ext, wg_ref[:, cols], preferred_element_type=F32)
        cw = cw_ref[:, cols]
        g = (gate_ref[HALO - 1:HALO - 1 + t, :] * cw[0:1]
             + gate_ref[HALO:HALO + t, :] * cw[1:2]
             + gate_ref[HALO + 1:HALO + 1 + t, :] * cw[2:3]
             + cw[3:4])
        up = jnp.dot(n, wu_ref[:, cols], preferred_element_type=F32)
        act = 0.5 * g * (1.0 + lax.erf(g * (2.0 ** -0.5)))
        u_ref[:, cols] = (act * up).astype(BF16)
    y = h_ref[0] + jnp.dot(u_ref[...], wd_ref[...], preferred_element_type=F32)
    ms = jnp.mean(y * y, axis=-1, keepdims=True)
    o_ref[0] = y * lax.rsqrt(ms + NORM_EPS) * gfin_ref[...]


def _ffn_call(h, n2, n2_meta, wg, wu, cw, wd, g_final, tile):
    b, s, d = h.shape
    grid = (b, s // tile)
    per = tile // HALO
    n_halo_blocks = s // HALO
    return pl.pallas_call(
        _ffn_kernel,
        grid=grid,
        in_specs=[
            pl.BlockSpec((1, tile, d), lambda bi, i: (bi, i, 0)),
            pl.BlockSpec((1, tile, d), lambda bi, i: (bi, i, 0)),
            pl.BlockSpec((1, HALO, d), lambda bi, i: (bi, jnp.maximum(i * per - 1, 0), 0)),
            pl.BlockSpec((1, HALO, d), lambda bi, i: (bi, 0, 0)),
            pl.BlockSpec((1, HALO, d),
                         lambda bi, i: (bi, jnp.minimum((i + 1) * per, n_halo_blocks - 1), 0)),
            _const_spec((d, D_FF)),
            _const_spec((d, D_FF)),
            _const_spec((8, D_FF)),
            _const_spec((D_FF, d)),
            _const_spec((1, d)),
        ],
        out_specs=pl.BlockSpec((1, tile, d), lambda bi, i: (bi, i, 0)),
        out_shape=jax.ShapeDtypeStruct((b, s, d), F32),
        scratch_shapes=[
            pltpu.VMEM((tile + 2 * HALO, FF_CHUNK), F32),
            pltpu.VMEM((tile + 2 * HALO, FF_CHUNK), F32),
            pltpu.VMEM((tile, D_FF), BF16),
        ],
        compiler_params=_params(2),
        name="ffn",
    )(h, n2, n2, n2_meta, n2, wg, wu, cw, wd, g_final)


def _pair_layout(ang):
    c = jnp.cos(ang)
    s = jnp.sin(ang)
    return jnp.tile(c, (1, 4)), jnp.concatenate([-s, -s, s, s], axis=-1)


def _linear_inv():
    return ROPE_THETA ** (-jnp.arange(0, HEAD_DIM, 2, dtype=F32) / HEAD_DIM)


def _real_tables(s):
    t = jnp.arange(s)
    rowp = (t // GRID_W).astype(F32)
    colp = (t % GRID_W).astype(F32)
    axis_dim = HEAD_DIM // 2
    inv_a = ROPE_THETA ** (-jnp.arange(0, axis_dim, 2, dtype=F32) / axis_dim)
    ang_a = jnp.concatenate([rowp[:, None] * inv_a[None], colp[:, None] * inv_a[None]], axis=-1)
    pos = jnp.arange(N_META + s, dtype=F32)[N_META:]
    ang_b = pos[:, None] * _linear_inv()[None]
    return _pair_layout(ang_a) + _pair_layout(ang_b)


def _meta_tables():
    ang_a = jnp.zeros((META_ROWS, HEAD_DIM // 2), F32)
    pos = jnp.arange(META_ROWS, dtype=F32)
    ang_b = pos[:, None] * _linear_inv()[None]
    return _pair_layout(ang_a) + _pair_layout(ang_b)


def _attn_step_plan(s, tq):
    kv_bytes = N_KV * s * LANES * 2
    per_tile = tq * D_MODEL * (2 * 2 * 4 + 2 * 2 * 2 + 2)

    def tiles(vt_buffers):
        fixed = 2 * s * 2 * tq * 4 + (2 + vt_buffers) * kv_bytes + ATTN_TEMP_BYTES
        n = 1
        while 2 * n * tq <= s and fixed + 2 * n * per_tile <= VMEM_LIMIT:
            n *= 2
        return n

    return (tiles(1), 1) if tiles(1) > tiles(2) else (tiles(2), 2)


def _trunk(x, meta_x, meta_qkv, prm, cfg):
    s = x.shape[1]
    q, k, vt = _proj_call(x, _real_tables(s), prm["w_in"], prm["g_mix"], prm["gq"], prm["gk"],
                          prm["bd"], min(s, cfg["proj_tile"]))
    q_m, k_m, vt_m = meta_qkv
    kc = min(s, cfg["key_chunk"])
    tq = min(s, cfg["tq"])
    n_tiles, vt_buffers = _attn_step_plan(s, tq)
    h1, n2 = _attn_call(x, q, k, vt, k_m, vt_m, prm["lam"], prm["g_subln"], prm["w_out"],
                        prm["g_ffn"], tq, n_tiles, vt_buffers, kc)
    n2_m = _meta_attn_call(meta_x[:, :N_META], q_m[:, :, :N_META], k, vt, k_m, vt_m, prm["lam"],
                           prm["g_subln"], prm["w_out"], prm["g_ffn"], kc)
    return _ffn_call(h1, n2, n2_m, prm["wg"], prm["wu"], prm["cw"], prm["wd"], prm["g_final"],
                     min(s, cfg["row_tile"]))


_CFG = dict(proj_tile=1024, tq=256, key_chunk=512, row_tile=1024)


def kernel(x_prompt, x_sample, meta_tokens, g_mix, w_in, g_qnorm_a, g_knorm_a, lambda_q1, lambda_k1,
           lambda_q2, lambda_k2, g_subln, w_out, g_ffn, w_ff_gate, w_ff_up, conv_w, conv_b,
           w_ff_down, g_final):
    assert w_in.shape[0] == 1, "single-layer trunk"
    d = D_MODEL
    head_order = np.array([0, 4, 1, 5, 2, 6, 3, 7])
    a_cols = (head_order[:, None] * HEAD_DIM + np.arange(HEAD_DIM)[None]).reshape(-1)
    in_perm = np.concatenate([a_cols, np.arange(A_PAIRS * LANES, IN_COLS)])
    out_perm = np.concatenate([a_cols, np.arange(A_PAIRS * LANES, d)])
    half = HEAD_DIM // 2
    tile_perm = np.concatenate([np.arange(0, half), np.arange(2 * half, 3 * half),
                                np.arange(half, 2 * half), np.arange(3 * half, 4 * half)])
    qk_tiles = list(range(0, 5)) + list(range(6, 14))
    for tile in qk_tiles:
        in_perm[tile * LANES:(tile + 1) * LANES] = in_perm[tile * LANES + tile_perm]
    lane_dim = np.concatenate([np.arange(half), np.arange(half),
                               np.arange(half, 2 * half), np.arange(half, 2 * half)])
    blk = (np.arange(LANES) // half) % 2
    cw = jnp.concatenate([conv_w[0], conv_b[0][None], jnp.zeros((4, D_FF), F32)], axis=0)
    lam = jnp.stack([lambda_q1[0], lambda_k1[0], lambda_q2[0], lambda_k2[0]])
    prm = dict(
        w_in=w_in[0][:, in_perm].astype(BF16),
        g_mix=g_mix[0][None],
        gq=g_qnorm_a[0][lane_dim][None],
        gk=g_knorm_a[0][lane_dim][None],
        bd=jnp.asarray((blk[:, None] == blk[None]) / HEAD_DIM, BF16),
        lam=jnp.zeros((8, LANES), F32).at[:4, :HEAD_DIM].set(lam),
        g_subln=g_subln[0][None],
        w_out=w_out[0][out_perm].astype(BF16),
        g_ffn=g_ffn[0][None],
        wg=w_ff_gate[0].astype(BF16),
        wu=w_ff_up[0].astype(BF16),
        cw=cw,
        wd=w_ff_down[0].astype(BF16),
        g_final=g_final[None],
    )
    meta_x = jnp.zeros((1, META_ROWS, d), F32).at[0, :N_META].set(meta_tokens)
    meta_qkv = _proj_call(meta_x, _meta_tables(), prm["w_in"], prm["g_mix"], prm["gq"], prm["gk"],
                          prm["bd"], META_ROWS)
    y_prompt = _trunk(x_prompt, meta_x, meta_qkv, prm, _CFG)
    y_sample = _trunk(x_sample, meta_x, meta_qkv, prm, _CFG)
    return (y_prompt, y_sample)
```

```python
import functools
import math

import jax
import jax.numpy as jnp
import numpy as np
from jax import lax
from jax.experimental import pallas as pl
from jax.experimental.pallas import tpu as pltpu

F32 = jnp.float32
BF16 = jnp.bfloat16

D_MODEL = 1024
HEAD_DIM = 64
N_META = 16
GRID_W = 64
ROPE_THETA = 10000.0
NORM_EPS = 1e-6
LANES = 128
N_PAIRS = 8
N_KV = 5
A_PAIRS = 4
D_FF = 2816
FF_CHUNK = 256
N_FF_CHUNKS = D_FF // FF_CHUNK
IN_COLS = 2304
HALO = 16
META_ROWS = 128
LAMBDA_INIT = 0.8 - 0.6 * math.exp(-0.3 * 0)
Q_SCALE = (HEAD_DIM ** -0.5) * math.log2(math.e)
VMEM_LIMIT = 56 * 1024 * 1024
ATTN_TEMP_BYTES = 11 * 1024 * 1024


def _params(n_grid_axes):
    return pltpu.CompilerParams(
        dimension_semantics=("arbitrary",) * n_grid_axes,
        vmem_limit_bytes=VMEM_LIMIT,
    )


def _const_spec(shape):
    zeros = (0,) * len(shape)
    return pl.BlockSpec(shape, lambda *_: zeros, pipeline_mode=pl.Buffered(1))


def _proj_kernel(x_ref, gmix_ref, w_ref, gq_ref, gk_ref, bd_ref, ca_ref, sa_ref, cb_ref, sb_ref,
                 q_ref, k_ref, vt_ref):
    x = x_ref[0]
    t = x.shape[0]
    ms = jnp.mean(x * x, axis=-1, keepdims=True)
    n = (x * lax.rsqrt(ms + NORM_EPS) * gmix_ref[...]).astype(BF16)

    def rope(y, c, s):
        return y * c + pltpu.roll(y, LANES // 2, 1) * s

    def proj(col):
        y = jnp.dot(n, w_ref[:, col:col + 2 * LANES], preferred_element_type=F32)
        return y[:, :LANES], y[:, LANES:]

    ca, sa, cb, sb = ca_ref[...], sa_ref[...], cb_ref[...], sb_ref[...]
    gq, gk = gq_ref[...], gk_ref[...]

    ya = list(proj(0) + proj(2 * LANES))
    yk, yv = proj(4 * LANES)
    vt_ref[0, 0] = yv.T.astype(BF16)
    sq = jnp.concatenate([(y * y).astype(BF16) for y in ya + [yk]], axis=0)
    ms_all = jnp.dot(sq, bd_ref[...], preferred_element_type=F32)
    for p in range(A_PAIRS):
        r = lax.rsqrt(ms_all[p * t:(p + 1) * t] + NORM_EPS) * Q_SCALE
        q_ref[0, p] = (rope(ya[p] * gq, ca, sa) * r).astype(BF16)
    r = lax.rsqrt(ms_all[A_PAIRS * t:] + NORM_EPS)
    k_ref[0, 0] = (rope(yk * gk, ca, sa) * r).astype(BF16)
    for j in range(2):
        y0, y1 = proj(14 * LANES + 2 * LANES * j)
        vt_ref[0, 1 + 2 * j] = y0.T.astype(BF16)
        vt_ref[0, 2 + 2 * j] = y1.T.astype(BF16)
    for j in range(2):
        y0, y1 = proj(10 * LANES + 2 * LANES * j)
        k_ref[0, 1 + 2 * j] = rope(y0, cb, sb).astype(BF16)
        k_ref[0, 2 + 2 * j] = rope(y1, cb, sb).astype(BF16)
    for j in range(2):
        y0, y1 = proj(6 * LANES + 2 * LANES * j)
        q_ref[0, A_PAIRS + 2 * j] = (rope(y0, cb, sb) * Q_SCALE).astype(BF16)
        q_ref[0, A_PAIRS + 2 * j + 1] = (rope(y1, cb, sb) * Q_SCALE).astype(BF16)


def _proj_call(x, tables, w_in, gmix, gq, gk, bd, tile):
    b, s, d = x.shape
    ca, sa, cb, sb = tables
    grid = (b, s // tile)
    tab_spec = pl.BlockSpec((tile, LANES), lambda bi, i: (i, 0))
    return pl.pallas_call(
        _proj_kernel,
        grid=grid,
        in_specs=[
            pl.BlockSpec((1, tile, d), lambda bi, i: (bi, i, 0)),
            _const_spec((1, d)),
            _const_spec((d, IN_COLS)),
            _const_spec((1, LANES)),
            _const_spec((1, LANES)),
            _const_spec((LANES, LANES)),
            tab_spec, tab_spec, tab_spec, tab_spec,
        ],
        out_specs=[
            pl.BlockSpec((1, N_PAIRS, tile, LANES), lambda bi, i: (bi, 0, i, 0)),
            pl.BlockSpec((1, N_KV, tile, LANES), lambda bi, i: (bi, 0, i, 0)),
            pl.BlockSpec((1, N_KV, LANES, tile), lambda bi, i: (bi, 0, 0, i)),
        ],
        out_shape=[
            jax.ShapeDtypeStruct((b, N_PAIRS, s, LANES), BF16),
            jax.ShapeDtypeStruct((b, N_KV, s, LANES), BF16),
            jax.ShapeDtypeStruct((b, N_KV, LANES, s), BF16),
        ],
        compiler_params=_params(2),
        name="proj",
    )(x, gmix, w_in, gq, gk, bd, ca, sa, cb, sb)


def _attn_kernel(x_ref, q_ref, k_ref, vt_ref, km_ref, vtm_ref, lam_ref, gsub_ref, wo_ref, gffn_ref,
                 h_ref, n_ref, sa_ref, sb_ref, acca_ref, accb_ref, la_ref, lb_ref, mix_ref,
                 *, tq, key_chunk):
    n_tiles = q_ref.shape[2] // tq
    s_len = k_ref.shape[2]
    n_chunks = s_len // key_chunk
    lane = lax.broadcasted_iota(jnp.int32, (tq, LANES), 1)
    q_low = (lane & (HEAD_DIM // 2)) == 0
    low = lane < HEAD_DIM
    contract_lanes = (((1,), (1,)), ((), ()))

    lam_p = lam_ref[...]
    lam = (jnp.exp(jnp.sum(lam_p[0:1] * lam_p[1:2], axis=-1, keepdims=True))
           - jnp.exp(jnp.sum(lam_p[2:3] * lam_p[3:4], axis=-1, keepdims=True))
           + LAMBDA_INIT)

    def chunk(c):
        return slice(c * key_chunk, (c + 1) * key_chunk)

    def locate(is_b, i):
        if isinstance(i, int):
            t, j = divmod(i, A_PAIRS)
        else:
            t, j = lax.shift_right_logical(i, 2), lax.bitwise_and(i, A_PAIRS - 1)
        return t, (j + A_PAIRS if is_b else j), (j + 1 if is_b else 0)

    def stage(scores_job=None, softmax_job=None, finish_job=None):
        if scores_job is not None:
            b1, i1, buf1 = scores_job
            t1, p1, kv1 = locate(b1, i1)
            row0 = t1 * tq if isinstance(t1, int) else pl.multiple_of(t1 * tq, tq)
            qp = q_ref[0, p1, pl.ds(row0, tq), :].astype(F32)
            qq = jnp.concatenate([jnp.where(q_low, qp, 0.0), jnp.where(q_low, 0.0, qp)],
                                 axis=0).astype(BF16)
            sm1 = lax.dot_general(km_ref[0, kv1][:N_META], qq, contract_lanes,
                                  preferred_element_type=F32)
            m8 = jnp.maximum(sm1[:8], sm1[8:])
        if softmax_job is not None:
            b2, i2, buf2, (sm2, m2) = softmax_job
            _, _, kv2 = locate(b2, i2)
            pm = jnp.exp2(sm2 - m2)
            l8 = pm[:8] + pm[8:]
            pm_pad = jnp.concatenate(
                [pm.astype(BF16), jnp.zeros((META_ROWS - N_META, 2 * tq), BF16)], axis=0)
            acc = jnp.dot(vtm_ref[0, kv2], pm_pad, preferred_element_type=F32)
        out = None
        if finish_job is not None:
            b3, i3 = finish_job
            t3, p3, _ = locate(b3, i3)
            acc_ref, l_ref = (accb_ref, lb_ref) if b3 else (acca_ref, la_ref)
            inv_l = 1.0 / jnp.sum(l_ref[...], axis=0, keepdims=True)
            ot = (acc_ref[...] * inv_l).T
            o_lo, o_hi = ot[:tq], ot[tq:]
            if b3:
                dt = o_lo - lam * o_hi
                msd = jnp.mean(dt * dt, axis=-1, keepdims=True)
                out = (dt * lax.rsqrt(msd + NORM_EPS) * gsub_ref[...]
                       * (1.0 - LAMBDA_INIT)).astype(BF16)
            else:
                out = jnp.where(low, o_lo, o_hi).astype(BF16)
            mix_ref[t3, p3] = out
        for c in range(n_chunks):
            if scores_job is not None:
                sc = lax.dot_general(k_ref[0, kv1, chunk(c), :], qq, contract_lanes,
                                     preferred_element_type=F32)
                buf1[chunk(c), :] = sc
                m8 = jnp.maximum(m8, jnp.max(sc.reshape(key_chunk // 8, 8, 2 * tq), axis=0))
            if softmax_job is not None:
                pc = jnp.exp2(buf2[chunk(c), :] - m2)
                l8 = l8 + jnp.sum(pc.reshape(key_chunk // 8, 8, 2 * tq), axis=0)
                acc = acc + jnp.dot(vt_ref[0, kv2, :, chunk(c)], pc.astype(BF16),
                                    preferred_element_type=F32)
        if softmax_job is not None:
            acc_ref, l_ref = (accb_ref, lb_ref) if b2 else (acca_ref, la_ref)
            acc_ref[...] = acc
            l_ref[...] = l8
        if scores_job is not None:
            return sm1, jnp.max(m8, axis=0, keepdims=True)
        return out

    grp_a, grp_b = False, True
    last = n_tiles * A_PAIRS - 1
    stats = stage(scores_job=(grp_a, 0, sa_ref))
    stats = stage(scores_job=(grp_b, 0, sb_ref), softmax_job=(grp_a, 0, sa_ref, stats))
    stats = stage(scores_job=(grp_a, 1, sa_ref), softmax_job=(grp_b, 0, sb_ref, stats),
                  finish_job=(grp_a, 0))

    def body(i, stats_a):
        stats_b = stage(scores_job=(grp_b, i, sb_ref), softmax_job=(grp_a, i, sa_ref, stats_a),
                        finish_job=(grp_b, i - 1))
        return stage(scores_job=(grp_a, i + 1, sa_ref), softmax_job=(grp_b, i, sb_ref, stats_b),
                     finish_job=(grp_a, i))

    stats = lax.fori_loop(0, (last - 1) // 2,
                          lambda k, st: body(2 * k + 2, body(2 * k + 1, st)), stats)
    stats = stage(scores_job=(grp_b, last, sb_ref), softmax_job=(grp_a, last, sa_ref, stats),
                  finish_job=(grp_b, last - 1))
    stage(softmax_job=(grp_b, last, sb_ref, stats), finish_job=(grp_a, last))
    stage(finish_job=(grp_b, last))

    for t in range(n_tiles):
        rows = slice(t * tq, (t + 1) * tq)
        mix = jnp.concatenate([mix_ref[t, p] for p in range(N_PAIRS)], axis=1)
        h = x_ref[0, rows, :] + jnp.dot(mix, wo_ref[...], preferred_element_type=F32)
        h_ref[0, rows, :] = h
        ms = jnp.mean(h * h, axis=-1, keepdims=True)
        n_ref[0, rows, :] = (h * lax.rsqrt(ms + NORM_EPS) * gffn_ref[...]).astype(BF16)


def _attn_call(x, q, k, vt, km, vtm, lam_p, gsub, w_out, g_ffn, tq, n_tiles, vt_buffers, key_chunk):
    b, _, s, _ = k.shape
    d = x.shape[-1]
    rows = tq * n_tiles
    grid = (b, s // rows)
    return pl.pallas_call(
        functools.partial(_attn_kernel, tq=tq, key_chunk=key_chunk),
        grid=grid,
        in_specs=[
            pl.BlockSpec((1, rows, d), lambda bi, i: (bi, i, 0)),
            pl.BlockSpec((1, N_PAIRS, rows, LANES), lambda bi, i: (bi, 0, i, 0)),
            pl.BlockSpec((1, N_KV, s, LANES), lambda bi, i: (bi, 0, 0, 0)),
            pl.BlockSpec((1, N_KV, LANES, s), lambda bi, i: (bi, 0, 0, 0),
                         pipeline_mode=pl.Buffered(vt_buffers)),
            _const_spec((1, N_KV, META_ROWS, LANES)),
            _const_spec((1, N_KV, LANES, META_ROWS)),
            _const_spec((8, LANES)),
            _const_spec((1, LANES)),
            _const_spec((d, d)),
            _const_spec((1, d)),
        ],
        out_specs=[
            pl.BlockSpec((1, rows, d), lambda bi, i: (bi, i, 0)),
            pl.BlockSpec((1, rows, d), lambda bi, i: (bi, i, 0)),
        ],
        out_shape=[
            jax.ShapeDtypeStruct((b, s, d), F32),
            jax.ShapeDtypeStruct((b, s, d), BF16),
        ],
        scratch_shapes=[
            pltpu.VMEM((s, 2 * tq), F32),
            pltpu.VMEM((s, 2 * tq), F32),
            pltpu.VMEM((LANES, 2 * tq), F32),
            pltpu.VMEM((LANES, 2 * tq), F32),
            pltpu.VMEM((8, 2 * tq), F32),
            pltpu.VMEM((8, 2 * tq), F32),
            pltpu.VMEM((n_tiles, N_PAIRS, tq, LANES), BF16),
        ],
        compiler_params=_params(2),
        name="attn",
    )(x, q, k, vt, km, vtm, lam_p, gsub, w_out, g_ffn)


def _meta_attn_kernel(x_ref, q_ref, k_ref, vt_ref, km_ref, vtm_ref, lam_ref, gsub_ref, wo_ref,
                      gffn_ref, n_ref, s0_ref, s1_ref, *, key_chunk):
    s_len = k_ref.shape[2]
    n_chunks = s_len // key_chunk
    lane = lax.broadcasted_iota(jnp.int32, (N_META, LANES), 1)
    q_low = (lane & (HEAD_DIM // 2)) == 0
    low = lane < HEAD_DIM
    contract_lanes = (((1,), (1,)), ((), ()))

    lam_p = lam_ref[...]
    lam = (jnp.exp(jnp.sum(lam_p[0:1] * lam_p[1:2], axis=-1, keepdims=True))
           - jnp.exp(jnp.sum(lam_p[2:3] * lam_p[3:4], axis=-1, keepdims=True))
           + LAMBDA_INIT)

    def maps(p):
        qp = q_ref[0, p].astype(F32)
        return [jnp.where(q_low, qp, 0.0), jnp.where(q_low, 0.0, qp)]

    def cat(parts, axis):
        return parts[0] if len(parts) == 1 else jnp.concatenate(parts, axis=axis)

    def attend(q_rows, kvs, s_ref):
        qq = q_rows.astype(BF16)
        sm = lax.dot_general(cat([km_ref[0, kv][:N_META] for kv in kvs], 1), qq, contract_lanes,
                             preferred_element_type=F32)
        m8 = jnp.maximum(sm[:8], sm[8:])
        for c in range(n_chunks):
            rows = slice(c * key_chunk, (c + 1) * key_chunk)
            sc = lax.dot_general(cat([k_ref[0, kv, rows, :] for kv in kvs], 1), qq,
                                 contract_lanes, preferred_element_type=F32)
            s_ref[rows, :] = sc
            m8 = jnp.maximum(m8, jnp.max(sc.reshape(key_chunk // 8, 8, LANES), axis=0))
        m = jnp.max(m8, axis=0, keepdims=True)
        pm = jnp.exp2(sm - m)
        l8 = pm[:8] + pm[8:]
        pm_pad = jnp.concatenate(
            [pm.astype(BF16), jnp.zeros((META_ROWS - N_META, LANES), BF16)], axis=0)
        acc = jnp.dot(cat([vtm_ref[0, kv] for kv in kvs], 0), pm_pad,
                      preferred_element_type=F32)
        for c in range(n_chunks):
            rows = slice(c * key_chunk, (c + 1) * key_chunk)
            pc = jnp.exp2(s_ref[rows, :] - m)
            l8 = l8 + jnp.sum(pc.reshape(key_chunk // 8, 8, LANES), axis=0)
            acc = acc + jnp.dot(cat([vt_ref[0, kv, :, rows] for kv in kvs], 0), pc.astype(BF16),
                                preferred_element_type=F32)
        inv_l = 1.0 / jnp.sum(l8, axis=0, keepdims=True)
        return [(acc[LANES * j:LANES * (j + 1)] * inv_l).T for j in range(len(kvs))]

    pieces = []
    ot, = attend(jnp.concatenate([mp for p in range(A_PAIRS) for mp in maps(p)], axis=0), (0,),
                 s0_ref)
    for p in range(A_PAIRS):
        o_lo = ot[2 * N_META * p:2 * N_META * p + N_META]
        o_hi = ot[2 * N_META * p + N_META:2 * N_META * (p + 1)]
        pieces.append(jnp.where(low, o_lo, o_hi).astype(BF16))
    zeros = jnp.zeros((2 * N_META, LANES), F32)
    idle = jnp.zeros((LANES - 4 * N_META, 2 * LANES), F32)
    for jb in range((N_PAIRS - A_PAIRS) // 2):
        hd = (2 * jb, 2 * jb + 1)
        first = jnp.concatenate([jnp.concatenate(maps(A_PAIRS + hd[0]), axis=0), zeros], axis=1)
        second = jnp.concatenate([zeros, jnp.concatenate(maps(A_PAIRS + hd[1]), axis=0)], axis=1)
        outs = attend(jnp.concatenate([first, second, idle], axis=0), (hd[0] + 1, hd[1] + 1),
                      s1_ref if jb % 2 == 0 else s0_ref)
        for j, ot in enumerate(outs):
            base = 2 * N_META * j
            dt = ot[base:base + N_META] - lam * ot[base + N_META:base + 2 * N_META]
            msd = jnp.mean(dt * dt, axis=-1, keepdims=True)
            pieces.append((dt * lax.rsqrt(msd + NORM_EPS) * gsub_ref[...]
                           * (1.0 - LAMBDA_INIT)).astype(BF16))
    mix = jnp.concatenate(pieces, axis=1)
    h = x_ref[0] + jnp.dot(mix, wo_ref[...], preferred_element_type=F32)
    ms = jnp.mean(h * h, axis=-1, keepdims=True)
    n_ref[0] = (h * lax.rsqrt(ms + NORM_EPS) * gffn_ref[...]).astype(BF16)


def _meta_attn_call(x_meta, q_meta, k, vt, km, vtm, lam_p, gsub, w_out, g_ffn, key_chunk):
    b, _, s, _ = k.shape
    d = x_meta.shape[-1]
    return pl.pallas_call(
        functools.partial(_meta_attn_kernel, key_chunk=key_chunk),
        grid=(b,),
        in_specs=[
            _const_spec((1, N_META, d)),
            _const_spec((1, N_PAIRS, N_META, LANES)),
            pl.BlockSpec((1, N_KV, s, LANES), lambda bi: (bi, 0, 0, 0)),
            pl.BlockSpec((1, N_KV, LANES, s), lambda bi: (bi, 0, 0, 0)),
            _const_spec((1, N_KV, META_ROWS, LANES)),
            _const_spec((1, N_KV, LANES, META_ROWS)),
            _const_spec((8, LANES)),
            _const_spec((1, LANES)),
            _const_spec((d, d)),
            _const_spec((1, d)),
        ],
        out_specs=pl.BlockSpec((1, N_META, d), lambda bi: (bi, 0, 0)),
        out_shape=jax.ShapeDtypeStruct((b, N_META, d), BF16),
        scratch_shapes=[pltpu.VMEM((s, LANES), F32), pltpu.VMEM((s, LANES), F32)],
        compiler_params=_params(1),
        name="meta_attn",
    )(x_meta, q_meta, k, vt, km, vtm, lam_p, gsub, w_out, g_ffn)


def _ffn_kernel(h_ref, n_ref, left_ref, meta_ref, right_ref, wg_ref, wu_ref, cw_ref, wd_ref,
                gfin_ref, o_ref, gate0_ref, gate1_ref, u_ref):
    i = pl.program_id(1)
    last = pl.num_programs(1) - 1
    t = n_ref.shape[1]
    n = n_ref[0]
    left = jnp.where(i == 0, meta_ref[0], left_ref[0])
    right = jnp.where(i == last, jnp.zeros_like(right_ref[0]), right_ref[0])
    n_ext = jnp.concatenate([left, n, right], axis=0)

    for c in range(N_FF_CHUNKS):
        gate_ref = gate1_ref if c % 2 else gate0_ref
        cols = slice(c * FF_CHUNK, (c + 1) * FF_CHUNK)
        gate_ref[...] = jnp.dot(n_ext, wg_ref[:, cols], preferred_element_type=F32)
        cw = cw_ref[:, cols]
        g = (gate_ref[HALO - 1:HALO - 1 + t, :] * cw[0:1]
             + gate_ref[HALO:HALO + t, :] * cw[1:2]
             + gate_ref[HALO + 1:HALO + 1 + t, :] * cw[2:3]
             + cw[3:4])
        up = jnp.dot(n, wu_ref[:, cols], preferred_element_type=F32)
        act = 0.5 * g * (1.0 + lax.erf(g * (2.0 ** -0.5)))
        u_ref[:, cols] = (act * up).astype(BF16)
    y = h_ref[0] + jnp.dot(u_ref[...], wd_ref[...], preferred_element_type=F32)
    ms = jnp.mean(y * y, axis=-1, keepdims=True)
    o_ref[0] = y * lax.rsqrt(ms + NORM_EPS) * gfin_ref[...]


def _ffn_call(h, n2, n2_meta, wg, wu, cw, wd, g_final, tile):
    b, s, d = h.shape
    grid = (b, s // tile)
    per = tile // HALO
    n_halo_blocks = s // HALO
    return pl.pallas_call(
        _ffn_kernel,
        grid=grid,
        in_specs=[
            pl.BlockSpec((1, tile, d), lambda bi, i: (bi, i, 0)),
            pl.BlockSpec((1, tile, d), lambda bi, i: (bi, i, 0)),
            pl.BlockSpec((1, HALO, d), lambda bi, i: (bi, jnp.maximum(i * per - 1, 0), 0)),
            pl.BlockSpec((1, HALO, d), lambda bi, i: (bi, 0, 0)),
            pl.BlockSpec((1, HALO, d),
                         lambda bi, i: (bi, jnp.minimum((i + 1) * per, n_halo_blocks - 1), 0)),
            _const_spec((d, D_FF)),
            _const_spec((d, D_FF)),
            _const_spec((8, D_FF)),
            _const_spec((D_FF, d)),
            _const_spec((1, d)),
        ],
        out_specs=pl.BlockSpec((1, tile, d), lambda bi, i: (bi, i, 0)),
        out_shape=jax.ShapeDtypeStruct((b, s, d), F32),
        scratch_shapes=[
            pltpu.VMEM((tile + 2 * HALO, FF_CHUNK), F32),
            pltpu.VMEM((tile + 2 * HALO, FF_CHUNK), F32),
            pltpu.VMEM((tile, D_FF), BF16),
        ],
        compiler_params=_params(2),
        name="ffn",
    )(h, n2, n2, n2_meta, n2, wg, wu, cw, wd, g_final)


def _pair_layout(ang):
    c = jnp.cos(ang)
    s = jnp.sin(ang)
    return jnp.tile(c, (1, 4)), jnp.concatenate([-s, -s, s, s], axis=-1)


def _linear_inv():
    return ROPE_THETA ** (-jnp.arange(0, HEAD_DIM, 2, dtype=F32) / HEAD_DIM)


def _real_tables(s):
    t = jnp.arange(s)
    rowp = (t // GRID_W).astype(F32)
    colp = (t % GRID_W).astype(F32)
    axis_dim = HEAD_DIM // 2
    inv_a = ROPE_THETA ** (-jnp.arange(0, axis_dim, 2, dtype=F32) / axis_dim)
    ang_a = jnp.concatenate([rowp[:, None] * inv_a[None], colp[:, None] * inv_a[None]], axis=-1)
    pos = jnp.arange(N_META + s, dtype=F32)[N_META:]
    ang_b = pos[:, None] * _linear_inv()[None]
    return _pair_layout(ang_a) + _pair_layout(ang_b)


def _meta_tables():
    ang_a = jnp.zeros((META_ROWS, HEAD_DIM // 2), F32)
    pos = jnp.arange(META_ROWS, dtype=F32)
    ang_b = pos[:, None] * _linear_inv()[None]
    return _pair_layout(ang_a) + _pair_layout(ang_b)


def _attn_step_plan(s, tq):
    kv_bytes = N_KV * s * LANES * 2
    per_tile = tq * D_MODEL * (2 * 2 * 4 + 2 * 2 * 2 + 2)

    def tiles(vt_buffers):
        fixed = 2 * s * 2 * tq * 4 + (2 + vt_buffers) * kv_bytes + ATTN_TEMP_BYTES
        n = 1
        while 2 * n * tq <= s and fixed + 2 * n * per_tile <= VMEM_LIMIT:
            n *= 2
        return n

    return (tiles(1), 1) if tiles(1) > tiles(2) else (tiles(2), 2)


def _trunk(x, meta_x, meta_qkv, prm, cfg):
    s = x.shape[1]
    q, k, vt = _proj_call(x, _real_tables(s), prm["w_in"], prm["g_mix"], prm["gq"], prm["gk"],
                          prm["bd"], min(s, cfg["proj_tile"]))
    q_m, k_m, vt_m = meta_qkv
    kc = min(s, cfg["key_chunk"])
    tq = min(s, cfg["tq"])
    n_tiles, vt_buffers = _attn_step_plan(s, tq)
    h1, n2 = _attn_call(x, q, k, vt, k_m, vt_m, prm["lam"], prm["g_subln"], prm["w_out"],
                        prm["g_ffn"], tq, n_tiles, vt_buffers, kc)
    n2_m = _meta_attn_call(meta_x[:, :N_META], q_m[:, :, :N_META], k, vt, k_m, vt_m, prm["lam"],
                           prm["g_subln"], prm["w_out"], prm["g_ffn"], kc)
    return _ffn_call(h1, n2, n2_m, prm["wg"], prm["wu"], prm["cw"], prm["wd"], prm["g_final"],
                     min(s, cfg["row_tile"]))


_CFG = dict(proj_tile=1024, tq=256, key_chunk=512, row_tile=1024)


def kernel(x_prompt, x_sample, meta_tokens, g_mix, w_in, g_qnorm_a, g_knorm_a, lambda_q1, lambda_k1,
           lambda_q2, lambda_k2, g_subln, w_out, g_ffn, w_ff_gate, w_ff_up, conv_w, conv_b,
           w_ff_down, g_final):
    assert w_in.shape[0] == 1, "single-layer trunk"
    d = D_MODEL
    head_order = np.array([0, 4, 1, 5, 2, 6, 3, 7])
    a_cols = (head_order[:, None] * HEAD_DIM + np.arange(HEAD_DIM)[None]).reshape(-1)
    in_perm = np.concatenate([a_cols, np.arange(A_PAIRS * LANES, IN_COLS)])
    out_perm = np.concatenate([a_cols, np.arange(A_PAIRS * LANES, d)])
    half = HEAD_DIM // 2
    tile_perm = np.concatenate([np.arange(0, half), np.arange(2 * half, 3 * half),
                                np.arange(half, 2 * half), np.arange(3 * half, 4 * half)])
    qk_tiles = list(range(0, 5)) + list(range(6, 14))
    for tile in qk_tiles:
        in_perm[tile * LANES:(tile + 1) * LANES] = in_perm[tile * LANES + tile_perm]
    lane_dim = np.concatenate([np.arange(half), np.arange(half),
                               np.arange(half, 2 * half), np.arange(half, 2 * half)])
    blk = (np.arange(LANES) // half) % 2
    cw = jnp.concatenate([conv_w[0], conv_b[0][None], jnp.zeros((4, D_FF), F32)], axis=0)
    lam = jnp.stack([lambda_q1[0], lambda_k1[0], lambda_q2[0], lambda_k2[0]])
    prm = dict(
        w_in=w_in[0][:, in_perm].astype(BF16),
        g_mix=g_mix[0][None],
        gq=g_qnorm_a[0][lane_dim][None],
        gk=g_knorm_a[0][lane_dim][None],
        bd=jnp.asarray((blk[:, None] == blk[None]) / HEAD_DIM, BF16),
        lam=jnp.zeros((8, LANES), F32).at[:4, :HEAD_DIM].set(lam),
        g_subln=g_subln[0][None],
        w_out=w_out[0][out_perm].astype(BF16),
        g_ffn=g_ffn[0][None],
        wg=w_ff_gate[0].astype(BF16),
        wu=w_ff_up[0].astype(BF16),
        cw=cw,
        wd=w_ff_down[0].astype(BF16),
        g_final=g_final[None],
    )
    meta_x = jnp.zeros((1, META_ROWS, d), F32).at[0, :N_META].set(meta_tokens)
    meta_qkv = _proj_call(meta_x, _meta_tables(), prm["w_in"], prm["g_mix"], prm["gq"], prm["gk"],
                          prm["bd"], META_ROWS)
    y_prompt = _trunk(x_prompt, meta_x, meta_qkv, prm, _CFG)
    y_sample = _trunk(x_sample, meta_x, meta_qkv, prm, _CFG)
    return (y_prompt, y_sample)
```
